```python
import functools
import jax
import jax.numpy as jnp
from jax import lax
import numpy as np

D_MODEL = 1024
BATCH = 2
SEQ = 8192
DEPTH = 4
DEC_BATCH = 128
DEC_SEQ = 1
PAST_LEN = 2048
PAGE_SIZE = 128

N_A_LAYERS = DEPTH // 2
N_B_LAYERS = DEPTH - N_A_LAYERS
RET_HEADS = 4
RET_QK_DIM = D_MODEL // RET_HEADS
RET_V_DIM = 2 * RET_QK_DIM
RET_QK_WIDTH = RET_HEADS * RET_QK_DIM
RET_V_WIDTH = RET_HEADS * RET_V_DIM
RET_IN_WIDTH = 2 * RET_QK_WIDTH + 2 * RET_V_WIDTH
RET_CHUNK = 128
ROPE_BASE = 10000.0
FOX_HEADS = 16
FOX_HEAD_DIM = D_MODEL // FOX_HEADS
FOX_WIDTH = FOX_HEADS * FOX_HEAD_DIM
Q_BLOCK = 128
N_EXPERTS = 32
TOP_K = 4
D_FF = D_MODEL
SWIGLU_LIMIT = 7.0
SWIGLU_ALPHA = 1.702
NORM_EPS = 1e-6
N_MOD = 6
PAGE_POOL_NUM = 5
PAGE_POOL_DEN = 4
FORGET_BIAS = 4.0
MASK_VALUE = -1e30

kernel_name = 'yoco_retnet_fox_moe_step'


def rmsnorm(x, gain):
    xf = x.astype(jnp.float32)
    y = xf * lax.rsqrt(jnp.mean(xf * xf, axis=-1, keepdims=True) + NORM_EPS)
    return (y * gain.astype(jnp.float32)).astype(x.dtype)


def modulate(x, gain, shift, scale):
    return rmsnorm(x, gain) * (1.0 + scale[:, None, :]) + shift[:, None, :]


def ada_params(c, w, b, n):
    return jnp.split(jax.nn.silu(c) @ w + b, n, axis=-1)


def xpos_rotate(x, pos):
    half = x.shape[-1] // 2
    inv = 1.0 / (ROPE_BASE ** jnp.linspace(0.0, 1.0, half, dtype=jnp.float32))
    ang = jnp.repeat(pos.astype(jnp.float32)[:, None] * inv[None, :], 2, axis=-1)[:, None, :]
    x1 = x[..., 0::2]
    x2 = x[..., 1::2]
    rot = jnp.stack([-x2, x1], axis=-1).reshape(x.shape)
    return x * jnp.cos(ang) + rot * jnp.sin(ang)


def retention_log_decay():
    return jnp.log1p(-jnp.exp2(-5.0 - jnp.arange(RET_HEADS, dtype=jnp.float32)))


def retention_chunk(state, qkv, log_gamma):
    q, k, v = qkv
    c = q.shape[2]
    idx = jnp.arange(c, dtype=jnp.float32)
    diff = idx[:, None] - idx[None, :]
    decay = jnp.where(diff >= 0, jnp.exp(jnp.maximum(diff, 0.0)[None] * log_gamma[:, None, None]), 0.0)
    inner = jnp.einsum('bhik,bhjk->bhij', q, k) * decay
    q_dec = q * jnp.exp((idx + 1.0)[None, :] * log_gamma[:, None])[None, :, :, None]
    out = jnp.einsum('bhij,bhjv->bhiv', inner, v) + jnp.einsum('bhik,bhkv->bhiv', q_dec, state)
    k_dec = k * jnp.exp((c - 1.0 - idx)[None, :] * log_gamma[:, None])[None, :, :, None]
    new_state = jnp.exp(c * log_gamma)[None, :, None, None] * state + jnp.einsum('bhjk,bhjv->bhkv', k_dec, v)
    return new_state, out


def retention_mixer(h, w_in, w_out, state0, pos, chunk):
    b, s, _ = h.shape
    proj = h @ w_in
    q, k, v, g = jnp.split(proj, [RET_QK_WIDTH, 2 * RET_QK_WIDTH, 2 * RET_QK_WIDTH + RET_V_WIDTH], axis=-1)
    q = xpos_rotate(q.reshape(b, s, RET_HEADS, RET_QK_DIM).astype(jnp.float32), pos)
    k = xpos_rotate(k.reshape(b, s, RET_HEADS, RET_QK_DIM).astype(jnp.float32), pos) * RET_QK_DIM ** -0.5
    v = v.reshape(b, s, RET_HEADS, RET_V_DIM).astype(jnp.float32)
    n_chunks = s // chunk

    def to_chunks(t):
        return t.reshape(b, n_chunks, chunk, RET_HEADS, t.shape[-1]).transpose(1, 0, 3, 2, 4)

    step = functools.partial(retention_chunk, log_gamma=retention_log_decay())
    state, out = lax.scan(step, state0.astype(jnp.float32), (to_chunks(q), to_chunks(k), to_chunks(v)))
    out = out.transpose(1, 0, 3, 2, 4).reshape(b, s, RET_HEADS, RET_V_DIM)
    out = out * lax.rsqrt(jnp.mean(out * out, axis=-1, keepdims=True) + NORM_EPS)
    out = (jax.nn.silu(g) * out.reshape(b, s, RET_V_WIDTH).astype(h.dtype)) @ w_out
    return out, state.astype(state0.dtype)


def moe_ffn(h, w_r, b_r, w_gu, b_gu, w_dn, b_dn):
    b, s, d = h.shape
    t = h.reshape(b * s, d)
    logits = (t @ w_r + b_r).astype(jnp.float32)
    top_vals, top_idx = lax.top_k(logits, TOP_K)
    top_w = jax.nn.softmax(top_vals, axis=-1)
    gates = jnp.sum(jax.nn.one_hot(top_idx, N_EXPERTS, dtype=jnp.float32) * top_w[..., None], axis=1).astype(h.dtype)
    y = jnp.zeros_like(t)
    for e in range(N_EXPERTS):
        gu = t @ w_gu[e] + b_gu[e]
        glu = jnp.minimum(gu[:, 0::2], SWIGLU_LIMIT)
        lin = jnp.clip(gu[:, 1::2], -SWIGLU_LIMIT, SWIGLU_LIMIT)
        act = glu * jax.nn.sigmoid(SWIGLU_ALPHA * glu) * (lin + 1.0)
        y = y + gates[:, e:e + 1] * (act @ w_dn[e] + b_dn[e])
    return y.reshape(b, s, d)


def shared_kv(x, c, norm_kv, ada_kv_w, ada_kv_b, kv_w, f_w, f_b):
    b, s, _ = x.shape
    shift, scale = ada_params(c, ada_kv_w, ada_kv_b, 2)
    h = modulate(x, norm_kv, shift, scale)
    k, v = jnp.split(h @ kv_w, 2, axis=-1)
    logf = jax.nn.log_sigmoid((h @ f_w + f_b).astype(jnp.float32))
    return (k.reshape(b, s, FOX_HEADS, FOX_HEAD_DIM), v.reshape(b, s, FOX_HEADS, FOX_HEAD_DIM), logf)


def fox_logits(q, k, f_q, f_k, q_pos, k_pos):
    s = jnp.einsum('bqhd,bkhd->bhqk', q, k).astype(jnp.float32) * FOX_HEAD_DIM ** -0.5
    s = s + jnp.transpose(f_q, (0, 2, 1))[..., :, None] - jnp.transpose(f_k, (0, 2, 1))[..., None, :]
    return jnp.where(k_pos[None, :] <= q_pos[:, None], s, MASK_VALUE)


def fox_prompt(q, k, v, fcum):
    b, s, h, d = q.shape
    k_pos = jnp.arange(s, dtype=jnp.int32)

    def block(i):
        start = i * Q_BLOCK
        q_blk = lax.dynamic_slice_in_dim(q, start, Q_BLOCK, axis=1)
        f_blk = lax.dynamic_slice_in_dim(fcum, start, Q_BLOCK, axis=1)
        q_pos = start + jnp.arange(Q_BLOCK, dtype=jnp.int32)
        p = jax.nn.softmax(fox_logits(q_blk, k, f_blk, fcum, q_pos, k_pos), axis=-1)
        return jnp.einsum('bhqk,bkhd->bqhd', p.astype(v.dtype), v)

    out = lax.map(block, jnp.arange(s // Q_BLOCK, dtype=jnp.int32))
    return out.transpose(1, 0, 2, 3, 4).reshape(b, s, h * d)


def fox_sample(q, k_new, v_new, logf_new, k_past, v_past, logf_past):
    b, t, h, d = q.shape
    p_len = k_past.shape[1]
    fcum = jnp.cumsum(jnp.concatenate([logf_past.astype(jnp.float32), logf_new], axis=1), axis=1)
    f_past = fcum[:, :p_len]
    f_new = fcum[:, p_len:]
    q_pos = p_len + jnp.arange(t, dtype=jnp.int32)
    s_past = fox_logits(q, k_past, f_new, f_past, q_pos, jnp.arange(p_len, dtype=jnp.int32))
    s_new = fox_logits(q, k_new, f_new, f_new, q_pos, q_pos)
    p = jax.nn.softmax(jnp.concatenate([s_past, s_new], axis=-1), axis=-1)
    out = (jnp.einsum('bhqk,bkhd->bqhd', p[..., :p_len].astype(v_past.dtype), v_past)
           + jnp.einsum('bhqk,bkhd->bqhd', p[..., p_len:].astype(v_new.dtype), v_new))
    return out.reshape(b, t, h * d)


def trunk(x, c, ret_state0, pos, chunk, attend, norm_mix, norm_ffn, ada_w, ada_b, ret_w_in, ret_w_out,
          fox_w_q, fox_w_o, norm_kv, ada_kv_w, ada_kv_b, kv_w, f_w, f_b, router_w, router_b,
          expert_w_gu, expert_b_gu, expert_w_down, expert_b_down, norm_final, ada_final_w, ada_final_b):
    b, s, _ = x.shape
    new_states = []
    k = v = logf = None
    for layer in range(DEPTH):
        sh1, sc1, g1, sh2, sc2, g2 = ada_params(c, ada_w[layer], ada_b[layer], N_MOD)
        h = modulate(x, norm_mix[layer], sh1, sc1)
        if layer < N_A_LAYERS:
            mix, st = retention_mixer(h, ret_w_in[layer], ret_w_out[layer], ret_state0[layer], pos, chunk)
            new_states.append(st)
        else:
            j = layer - N_A_LAYERS
            q = (h @ fox_w_q[j]).reshape(b, s, FOX_HEADS, FOX_HEAD_DIM)
            mix = attend(q, k, v, logf) @ fox_w_o[j]
        x = x + g1[:, None, :] * mix
        h = modulate(x, norm_ffn[layer], sh2, sc2)
        x = x + g2[:, None, :] * moe_ffn(h, router_w[layer], router_b[layer], expert_w_gu[layer],
                                           expert_b_gu[layer], expert_w_down[layer], expert_b_down[layer])
        if layer == N_A_LAYERS - 1:
            k, v, logf = shared_kv(x, c, norm_kv, ada_kv_w, ada_kv_b, kv_w, f_w, f_b)
    shift, scale = ada_params(c, ada_final_w, ada_final_b, 2)
    y = modulate(x, norm_final, shift, scale)
    return y, jnp.stack(new_states), k, v, logf


def setup_inputs(seed: int = 0) -> dict:
    key = jax.random.key(seed)
    ks = jax.random.split(key, 40)
    f32 = jnp.float32

    def nrm(k, shape, scale=1.0):
        return jax.random.normal(k, shape, f32) * scale

    n_pages = PAST_LEN // PAGE_SIZE
    n_used = DEC_BATCH * n_pages
    n_phys = (n_used * PAGE_POOL_NUM) // PAGE_POOL_DEN
    page_table = jax.random.permutation(ks[0], n_phys)[:n_used].reshape(DEC_BATCH, n_pages).astype(jnp.int32)
    d = D_MODEL
    return {
        'x_prompt': nrm(ks[1], (BATCH, SEQ, d)),
        'x_sample': nrm(ks[2], (DEC_BATCH, DEC_SEQ, d)),
        'state_ret': nrm(ks[3], (N_A_LAYERS, DEC_BATCH, RET_HEADS, RET_QK_DIM, RET_V_DIM), 0.5),
        'cache_k': nrm(ks[4], (n_phys, PAGE_SIZE, FOX_HEADS, FOX_HEAD_DIM)),
        'cache_v': nrm(ks[5], (n_phys, PAGE_SIZE, FOX_HEADS, FOX_HEAD_DIM)),
        'cache_logf': jax.nn.log_sigmoid(FORGET_BIAS + nrm(ks[6], (n_phys, PAGE_SIZE, FOX_HEADS))),
        'page_table': page_table,
        'c_prompt': nrm(ks[7], (BATCH, d)),
        'c_sample': nrm(ks[8], (DEC_BATCH, d)),
        'norm_mix': 1.0 + nrm(ks[9], (DEPTH, d), 0.02),
        'norm_ffn': 1.0 + nrm(ks[10], (DEPTH, d), 0.02),
        'ada_w': nrm(ks[11], (DEPTH, d, N_MOD * d), 0.5 * d ** -0.5),
        'ada_b': nrm(ks[12], (DEPTH, N_MOD * d), 0.02),
        'ret_w_in': nrm(ks[13], (N_A_LAYERS, d, RET_IN_WIDTH), d ** -0.5),
        'ret_w_out': nrm(ks[14], (N_A_LAYERS, RET_V_WIDTH, d), RET_V_WIDTH ** -0.5),
        'fox_w_q': nrm(ks[15], (N_B_LAYERS, d, FOX_WIDTH), d ** -0.5),
        'fox_w_o': nrm(ks[16], (N_B_LAYERS, FOX_WIDTH, d), FOX_WIDTH ** -0.5),
        'norm_kv': 1.0 + nrm(ks[17], (d,), 0.02),
        'ada_kv_w': nrm(ks[18], (d, 2 * d), 0.5 * d ** -0.5),
        'ada_kv_b': nrm(ks[19], (2 * d,), 0.02),
        'kv_w': nrm(ks[20], (d, 2 * FOX_WIDTH), d ** -0.5),
        'f_w': nrm(ks[21], (d, FOX_HEADS), 0.25 * d ** -0.5),
        'f_b': FORGET_BIAS + nrm(ks[22], (FOX_HEADS,), 1.0),
        'router_w': nrm(ks[23], (DEPTH, d, N_EXPERTS), d ** -0.5),
        'router_b': nrm(ks[24], (DEPTH, N_EXPERTS), 0.01),
        'expert_w_gu': nrm(ks[25], (DEPTH, N_EXPERTS, d, 2 * D_FF), d ** -0.5),
        'expert_b_gu': nrm(ks[26], (DEPTH, N_EXPERTS, 2 * D_FF), 0.01),
        'expert_w_down': nrm(ks[27], (DEPTH, N_EXPERTS, D_FF, d), D_FF ** -0.5),
        'expert_b_down': nrm(ks[28], (DEPTH, N_EXPERTS, d), 0.01),
        'norm_final': 1.0 + nrm(ks[29], (d,), 0.02),
        'ada_final_w': nrm(ks[30], (d, 2 * d), 0.5 * d ** -0.5),
        'ada_final_b': nrm(ks[31], (2 * d,), 0.02),
    }


def reference(x_prompt, x_sample, state_ret, cache_k, cache_v, cache_logf, page_table, c_prompt, c_sample,
              norm_mix, norm_ffn, ada_w, ada_b, ret_w_in, ret_w_out, fox_w_q, fox_w_o, norm_kv, ada_kv_w,
              ada_kv_b, kv_w, f_w, f_b, router_w, router_b, expert_w_gu, expert_b_gu, expert_w_down,
              expert_b_down, norm_final, ada_final_w, ada_final_b):
    weights = (norm_mix, norm_ffn, ada_w, ada_b, ret_w_in, ret_w_out, fox_w_q, fox_w_o, norm_kv, ada_kv_w,
               ada_kv_b, kv_w, f_w, f_b, router_w, router_b, expert_w_gu, expert_b_gu, expert_w_down,
               expert_b_down, norm_final, ada_final_w, ada_final_b)

    prompt_pos = jnp.arange(x_prompt.shape[1], dtype=jnp.int32)
    zero_state = jnp.zeros((N_A_LAYERS, x_prompt.shape[0], RET_HEADS, RET_QK_DIM, RET_V_DIM), state_ret.dtype)

    def attend_prompt(q, k, v, logf):
        return fox_prompt(q, k, v, jnp.cumsum(logf, axis=1))

    y_prompt, ret_prompt, k_prompt, v_prompt, logf_prompt = trunk(
        x_prompt, c_prompt, zero_state, prompt_pos, RET_CHUNK, attend_prompt, *weights)

    dec_batch, n_pages = page_table.shape
    past_len = n_pages * cache_k.shape[1]
    k_past = cache_k[page_table].reshape(dec_batch, past_len, FOX_HEADS, FOX_HEAD_DIM)
    v_past = cache_v[page_table].reshape(dec_batch, past_len, FOX_HEADS, FOX_HEAD_DIM)
    logf_past = cache_logf[page_table].reshape(dec_batch, past_len, FOX_HEADS)

    def attend_sample(q, k, v, logf):
        return fox_sample(q, k, v, logf, k_past, v_past, logf_past)

    sample_pos = past_len + jnp.arange(x_sample.shape[1], dtype=jnp.int32)
    y_sample, ret_sample, k_sample, v_sample, logf_sample = trunk(
        x_sample, c_sample, state_ret, sample_pos, x_sample.shape[1], attend_sample, *weights)

    return (y_prompt, y_sample, ret_prompt, ret_sample, k_prompt, v_prompt, logf_prompt, k_sample, v_sample, logf_sample)
```

```python
import functools

import jax
import jax.numpy as jnp
import numpy as np
from jax import lax
from jax.experimental import pallas as pl
from jax.experimental.pallas import tpu as pltpu

F32 = jnp.float32
BF16 = jnp.bfloat16

D_MODEL = 1024
DEPTH = 4
N_A_LAYERS = 2
RET_HEADS = 4
RET_QK_DIM = 256
RET_V_DIM = 512
RET_QK_WIDTH = 1024
RET_V_WIDTH = 2048
RET_CHUNK = 128
ROPE_BASE = 10000.0
FOX_HEADS = 16
FOX_HEAD_DIM = 64
N_EXPERTS = 32
TOP_K = 4
D_FF = 1024
SWIGLU_LIMIT = 7.0
SWIGLU_ALPHA = 1.702
NORM_EPS = 1e-6
N_MOD = 6
MASK_VALUE = -1e30

V7X_VMEM_LIMIT = 56 * 1024 * 1024
LANES = 128


def _cparams(sem):
    return pltpu.CompilerParams(dimension_semantics=sem, vmem_limit_bytes=V7X_VMEM_LIMIT)


def _dot(a, b):
    return jnp.dot(a, b, preferred_element_type=F32)


def _dot_nt(a, b):
    return lax.dot_general(a, b, (((1,), (1,)), ((), ())), preferred_element_type=F32)


def _dot_tn(a, b):
    return lax.dot_general(a, b, (((0,), (0,)), ((), ())), preferred_element_type=F32)


def _modulated(x, gain, shift, scale):
    xf = x.astype(F32)
    y = xf * lax.rsqrt(jnp.mean(xf * xf, axis=-1, keepdims=True) + NORM_EPS)
    return (y * gain) * (1.0 + scale) + shift


def _log_sigmoid(z):
    return jnp.minimum(z, 0.0) - jnp.log1p(jnp.exp(-jnp.abs(z)))


def _linear_body(*refs, mod, pre_silu, has_bias, epi, post, n_j, only_mod):
    it = iter(refs)
    x_ref = next(it)
    if mod:
        gain_ref, shift_ref, scale_ref = next(it), next(it), next(it)
    if not only_mod:
        w_ref = next(it)
    b_ref = next(it) if has_bias else None
    if epi:
        res_ref, gate_ref = next(it), next(it)
    o_ref = next(it)
    h_ref = next(it) if n_j > 1 else None

    def prologue():
        x = x_ref[...]
        if mod:
            return _modulated(x, gain_ref[...], shift_ref[0], scale_ref[0])
        if pre_silu:
            xf = x.astype(F32)
            return xf * jax.nn.sigmoid(xf)
        return x

    if only_mod:
        o_ref[...] = prologue().astype(o_ref.dtype)
        return

    if n_j > 1:
        @pl.when(pl.program_id(1) == 0)
        def _():
            h_ref[...] = prologue().astype(BF16)
        h = h_ref[...]
    else:
        h = prologue().astype(BF16)

    acc = _dot(h, w_ref[...])
    if has_bias:
        acc = acc + b_ref[...]
    if post == "log_sigmoid":
        acc = _log_sigmoid(acc)
    if epi:
        acc = res_ref[...] + gate_ref[0] * acc
    o_ref[...] = acc.astype(o_ref.dtype)


def _linear(x, w=None, *, mod=None, pre_silu=False, bias=None, epi=None, post=None,
            out_dtype=F32, tm=512, tn=1024, rows_per_group=None, name="linear"):
    m, k = x.shape
    only_mod = w is None
    n = k if only_mod else w.shape[1]
    tm = min(tm, m)
    tn = n if only_mod else min(tn, n)
    assert m % tm == 0 and n % tn == 0
    n_i, n_j = m // tm, n // tn
    if only_mod:
        n_j = 1

    def row_or_group(arr, width, col_tiled):
        if arr.ndim == 3:
            tiles_per_group = rows_per_group // tm
            assert rows_per_group % tm == 0
            if col_tiled:
                return arr, pl.BlockSpec((1, 1, tn), lambda i, j: (i // tiles_per_group, 0, j))
            return arr, pl.BlockSpec((1, 1, width), lambda i, j: (i // tiles_per_group, 0, 0))
        arr3 = arr.reshape(n_i, tm, width)
        if col_tiled:
            return arr3, pl.BlockSpec((1, tm, tn), lambda i, j: (i, 0, j))
        return arr3, pl.BlockSpec((1, tm, width), lambda i, j: (i, 0, 0))

    args = [x]
    specs = [pl.BlockSpec((tm, k), lambda i, j: (i, 0))]
    if mod is not None:
        gain, shift, scale = mod
        args.append(gain.reshape(1, k).astype(F32))
        specs.append(pl.BlockSpec((1, k), lambda i, j: (0, 0)))
        for a in (shift, scale):
            a3, sp = row_or_group(a, k, False)
            args.append(a3)
            specs.append(sp)
    if not only_mod:
        args.append(w)
        specs.append(pl.BlockSpec((k, tn), lambda i, j: (0, j)))
    if bias is not None:
        args.append(bias.reshape(1, n).astype(F32))
        specs.append(pl.BlockSpec((1, tn), lambda i, j: (0, j)))
    if epi is not None:
        res, gate = epi
        args.append(res)
        specs.append(pl.BlockSpec((tm, tn), lambda i, j: (i, j)))
        g3, sp = row_or_group(gate, n, True)
        args.append(g3)
        specs.append(sp)

    body = functools.partial(_linear_body, mod=mod is not None, pre_silu=pre_silu,
                             has_bias=bias is not None, epi=epi is not None, post=post,
                             n_j=n_j, only_mod=only_mod)
    scratch = [pltpu.VMEM((tm, k), BF16)] if n_j > 1 else []
    return pl.pallas_call(
        body,
        grid=(n_i, n_j),
        in_specs=specs,
        out_specs=pl.BlockSpec((tm, tn), lambda i, j: (i, j)),
        out_shape=jax.ShapeDtypeStruct((m, n), out_dtype),
        scratch_shapes=scratch,
        compiler_params=_cparams(("parallel", "arbitrary")),
        name=name,
    )(*args)


def _router_body(x_ref, gain_ref, shift_ref, scale_ref, wr_ref, br_ref, h_ref, ti_ref, tw_ref):
    h = _modulated(x_ref[...], gain_ref[...], shift_ref[0], scale_ref[0]).astype(BF16)
    h_ref[...] = h
    logits = _dot(h, wr_ref[...]) + br_ref[...]
    lane = lax.broadcasted_iota(jnp.int32, logits.shape, 1).astype(F32)
    vals, idxs = [], []
    cur = logits
    for _ in range(TOP_K):
        mx = jnp.max(cur, axis=-1, keepdims=True)
        idx = jnp.min(jnp.where(cur == mx, lane, float(LANES)), axis=-1, keepdims=True)
        vals.append(mx)
        idxs.append(idx)
        cur = jnp.where(lane == idx, -jnp.inf, cur)
    exps = [jnp.exp(v - vals[0]) for v in vals]
    denom = exps[0] + exps[1] + exps[2] + exps[3]
    ti = jnp.zeros(logits.shape, F32)
    tw = jnp.zeros(logits.shape, F32)
    for kk in range(TOP_K):
        ti = jnp.where(lane == float(kk), idxs[kk], ti)
        tw = jnp.where(lane == float(kk), exps[kk] / denom, tw)
    ti_ref[...] = ti.astype(jnp.int32)
    tw_ref[...] = tw


def _router(x, gain, shift, scale, wr_pad, br_pad, *, tm, rows_per_group=None):
    m, k = x.shape
    tm = min(tm, m)
    n_i = m // tm
    if shift.ndim == 3:
        tpg = rows_per_group // tm
        mod_spec = pl.BlockSpec((1, 1, k), lambda i: (i // tpg, 0, 0))
        sh3, sc3 = shift, scale
    else:
        mod_spec = pl.BlockSpec((1, tm, k), lambda i: (i, 0, 0))
        sh3, sc3 = shift.reshape(n_i, tm, k), scale.reshape(n_i, tm, k)
    return pl.pallas_call(
        _router_body,
        grid=(n_i,),
        in_specs=[pl.BlockSpec((tm, k), lambda i: (i, 0)),
                  pl.BlockSpec((1, k), lambda i: (0, 0)),
                  mod_spec, mod_spec,
                  pl.BlockSpec((k, LANES), lambda i: (0, 0)),
                  pl.BlockSpec((1, LANES), lambda i: (0, 0))],
        out_specs=[pl.BlockSpec((tm, k), lambda i: (i, 0)),
                   pl.BlockSpec((tm, LANES), lambda i: (i, 0)),
                   pl.BlockSpec((tm, LANES), lambda i: (i, 0))],
        out_shape=[jax.ShapeDtypeStruct((m, k), BF16),
                   jax.ShapeDtypeStruct((m, LANES), jnp.int32),
                   jax.ShapeDtypeStruct((m, LANES), F32)],
        compiler_params=_cparams(("parallel",)),
        name="moe_router",
    )(x, gain.reshape(1, k).astype(F32), sh3, sc3, wr_pad, br_pad)


MOE_TILE = 512


def _gmm_body(te_ref, tv_ref, hs_ref, w1_ref, b1_ref, w2_ref, b2_ref, o_ref):
    i = pl.program_id(0)

    @pl.when(tv_ref[i] > 0)
    def _():
        gu = _dot(hs_ref[...], w1_ref[0]) + b1_ref[0]
        glu = jnp.minimum(gu[:, :D_FF], SWIGLU_LIMIT)
        lin = jnp.clip(gu[:, D_FF:], -SWIGLU_LIMIT, SWIGLU_LIMIT)
        act = glu * jax.nn.sigmoid(SWIGLU_ALPHA * glu) * (lin + 1.0)
        y = _dot(act.astype(BF16), w2_ref[0]) + b2_ref[0]
        o_ref[...] = y.astype(o_ref.dtype)


def _moe_gmm(tile_expert, tile_valid, tile_block, hs, w1, b1, w2, b2):
    p_rows, d = hs.shape
    n_tiles = p_rows // MOE_TILE
    grid_spec = pltpu.PrefetchScalarGridSpec(
        num_scalar_prefetch=3,
        grid=(n_tiles,),
        in_specs=[
            pl.BlockSpec((MOE_TILE, d), lambda i, te, tv, tb: (tb[i], 0)),
            pl.BlockSpec((1, d, 2 * D_FF), lambda i, te, tv, tb: (te[i], 0, 0)),
            pl.BlockSpec((1, 1, 2 * D_FF), lambda i, te, tv, tb: (te[i], 0, 0)),
            pl.BlockSpec((1, D_FF, d), lambda i, te, tv, tb: (te[i], 0, 0)),
            pl.BlockSpec((1, 1, d), lambda i, te, tv, tb: (te[i], 0, 0)),
        ],
        out_specs=pl.BlockSpec((MOE_TILE, d), lambda i, te, tv, tb: (tb[i], 0)),
    )

    def body(te_ref, tv_ref, tb_ref, *rest):
        _gmm_body(te_ref, tv_ref, *rest)

    return pl.pallas_call(
        body,
        grid_spec=grid_spec,
        out_shape=jax.ShapeDtypeStruct((p_rows, d), BF16),
        compiler_params=_cparams(("arbitrary",)),
        name="moe_gmm",
    )(tile_expert, tile_valid, tile_block, hs, w1, b1, w2, b2)


def _moe_layer(x_parts, h_parts, topi, topw, gates2, w1, b1, w2, b2):
    h_all = jnp.concatenate(h_parts, axis=0)
    t = h_all.shape[0]
    n_slots = t * TOP_K
    flat_e = topi.reshape(n_slots)
    order = jnp.argsort(flat_e, stable=True).astype(jnp.int32)
    counts = jnp.sum(jax.nn.one_hot(flat_e, N_EXPERTS, dtype=jnp.int32), axis=0)
    offs = jnp.cumsum(counts) - counts
    padded = ((counts + MOE_TILE - 1) // MOE_TILE) * MOE_TILE
    poffs = jnp.cumsum(padded) - padded
    total_padded = jnp.sum(padded)
    p_rows = ((n_slots + N_EXPERTS * (MOE_TILE - 1)) // MOE_TILE + 1) * MOE_TILE
    n_tiles = p_rows // MOE_TILE
    tile_start = jnp.arange(n_tiles, dtype=jnp.int32) * MOE_TILE
    tile_valid = (tile_start < total_padded).astype(jnp.int32)
    tile_expert = jnp.clip(jnp.searchsorted(poffs, tile_start, side="right") - 1, 0, N_EXPERTS - 1).astype(jnp.int32)
    last_valid = jnp.maximum(total_padded // MOE_TILE - 1, 0).astype(jnp.int32)
    tile_block = jnp.minimum(jnp.arange(n_tiles, dtype=jnp.int32), last_valid)
    tile_expert = jnp.where(tile_valid > 0, tile_expert, tile_expert[last_valid])
    rows = jnp.arange(p_rows, dtype=jnp.int32)
    row_e = jnp.repeat(tile_expert, MOE_TILE)
    within = rows - poffs[row_e]
    row_ok = (within < counts[row_e]) & (jnp.repeat(tile_valid, MOE_TILE) > 0)
    src = jnp.clip(offs[row_e] + within, 0, n_slots - 1)
    row_slot = order[src]
    row_token = jnp.where(row_ok, row_slot // TOP_K, 0)
    inv = jnp.zeros((n_slots,), jnp.int32).at[jnp.where(row_ok, row_slot, n_slots)].set(rows, mode="drop")

    hs = jnp.take(h_all, row_token, axis=0)
    ys = _moe_gmm(tile_expert, tile_valid, tile_block, hs, w1, b1, w2, b2)
    yk = jnp.take(ys, inv, axis=0).reshape(t, TOP_K, -1).astype(F32)
    moe = jnp.sum(yk * topw[:, :, None], axis=1)
    outs, start = [], 0
    for xp, g2 in zip(x_parts, gates2):
        r = xp.shape[0]
        outs.append(xp + g2 * moe[start:start + r])
        start += r
    return outs


def _rope_tables(pos):
    half = RET_QK_DIM // 2
    inv = 1.0 / (ROPE_BASE ** jnp.linspace(0.0, 1.0, half, dtype=F32))
    ang = jnp.repeat(pos.astype(F32)[:, None] * inv[None, :], 2, axis=-1)
    cos, sin = jnp.cos(ang), jnp.sin(ang)
    even = (jnp.arange(RET_QK_DIM) % 2 == 0)[None, :]
    return cos, jnp.where(even, -sin, 0.0), jnp.where(even, 0.0, sin)


def _rope(x, cos, sin_next, sin_prev):
    n = x.shape[-1]
    return x * cos + pltpu.roll(x, n - 1, 1) * sin_next + pltpu.roll(x, 1, 1) * sin_prev


def _log_gamma():
    return jnp.log1p(-jnp.exp2(-5.0 - jnp.arange(RET_HEADS, dtype=F32)))


def _ret_prompt_body(q_ref, k_ref, v_ref, g_ref, cos_ref, sn_ref, sp_ref, dec_ref, rd_ref,
                     o_ref, st_ref, state):
    c = pl.program_id(2)

    @pl.when(c == 0)
    def _():
        state[...] = jnp.zeros_like(state)

    cos, sn, sp = cos_ref[...], sn_ref[...], sp_ref[...]
    q = _rope(q_ref[0].astype(F32), cos, sn, sp)
    k = _rope(k_ref[0].astype(F32), cos, sn, sp) * (RET_QK_DIM ** -0.5)
    v = v_ref[0]
    rd = rd_ref[0]
    q_dec = (q * rd[:, 0:1]).astype(BF16)
    k_dec = (k * rd[:, 1:2]).astype(BF16)
    inner = (_dot_nt(q.astype(BF16), k.astype(BF16)) * dec_ref[0]).astype(BF16)
    s_old = state[...]
    out = _dot(inner, v) + _dot(q_dec, s_old.astype(BF16))
    state[...] = rd[0:1, 2:3] * s_old + _dot_tn(k_dec, v)
    out = out * lax.rsqrt(jnp.mean(out * out, axis=-1, keepdims=True) + NORM_EPS)
    g = g_ref[0].astype(F32)
    o_ref[0] = (g * jax.nn.sigmoid(g) * out).astype(o_ref.dtype)

    @pl.when(c == pl.num_programs(2) - 1)
    def _():
        st_ref[0, 0] = state[...]


def _retention_prompt(proj, batch, seq):
    c = RET_CHUNK
    n_chunks = seq // c
    proj3 = proj.reshape(batch, seq, -1)
    cos, sn, sp = _rope_tables(jnp.arange(seq, dtype=jnp.int32))
    lg = _log_gamma()
    idx = jnp.arange(c, dtype=F32)
    diff = idx[:, None] - idx[None, :]
    decay = jnp.where(diff >= 0, jnp.exp(jnp.maximum(diff, 0.0)[None] * lg[:, None, None]), 0.0)
    qd = jnp.exp((idx + 1.0)[None, :] * lg[:, None])
    kd = jnp.exp((c - 1.0 - idx)[None, :] * lg[:, None])
    cd = jnp.broadcast_to(jnp.exp(c * lg)[:, None], (RET_HEADS, c))
    rd = jnp.zeros((RET_HEADS, c, LANES), F32).at[:, :, 0].set(qd).at[:, :, 1].set(kd).at[:, :, 2].set(cd)
    nqk = RET_QK_WIDTH // RET_QK_DIM
    out, st = pl.pallas_call(
        _ret_prompt_body,
        grid=(batch, RET_HEADS, n_chunks),
        in_specs=[
            pl.BlockSpec((1, c, RET_QK_DIM), lambda b, h, i: (b, i, h)),
            pl.BlockSpec((1, c, RET_QK_DIM), lambda b, h, i: (b, i, nqk + h)),
            pl.BlockSpec((1, c, RET_V_DIM), lambda b, h, i: (b, i, nqk + h)),
            pl.BlockSpec((1, c, RET_V_DIM), lambda b, h, i: (b, i, 2 * nqk + h)),
            pl.BlockSpec((c, RET_QK_DIM), lambda b, h, i: (i, 0)),
            pl.BlockSpec((c, RET_QK_DIM), lambda b, h, i: (i, 0)),
            pl.BlockSpec((c, RET_QK_DIM), lambda b, h, i: (i, 0)),
            pl.BlockSpec((1, c, c), lambda b, h, i: (h, 0, 0)),
            pl.BlockSpec((1, c, LANES), lambda b, h, i: (h, 0, 0)),
        ],
        out_specs=[
            pl.BlockSpec((1, c, RET_V_DIM), lambda b, h, i: (b, i, h)),
            pl.BlockSpec((1, 1, RET_QK_DIM, RET_V_DIM), lambda b, h, i: (b, h, 0, 0)),
        ],
        out_shape=[jax.ShapeDtypeStruct((batch, seq, RET_V_WIDTH), BF16),
                   jax.ShapeDtypeStruct((batch, RET_HEADS, RET_QK_DIM, RET_V_DIM), F32)],
        scratch_shapes=[pltpu.VMEM((RET_QK_DIM, RET_V_DIM), F32)],
        compiler_params=_cparams(("parallel", "parallel", "arbitrary")),
        name="retention_prompt",
    )(proj3, proj3, proj3, proj3, cos, sn, sp, decay, rd)
    return out.reshape(batch * seq, RET_V_WIDTH), st


def _ret_sample_body(p_ref, st_ref, rope_ref, gam_ref, o_ref, ns_ref):
    cos, sn, sp = rope_ref[0:1, :], rope_ref[1:2, :], rope_ref[2:3, :]
    row0 = lax.broadcasted_iota(jnp.int32, (8, RET_QK_DIM), 0) == 0
    for h in range(RET_HEADS):
        gamma = gam_ref[h:h + 1, 0:1]
        q = p_ref[0, :, h * RET_QK_DIM:(h + 1) * RET_QK_DIM].astype(F32)
        k = p_ref[0, :, RET_QK_WIDTH + h * RET_QK_DIM:RET_QK_WIDTH + (h + 1) * RET_QK_DIM].astype(F32)
        v0 = 2 * RET_QK_WIDTH + h * RET_V_DIM
        v = p_ref[0, :, v0:v0 + RET_V_DIM].astype(F32)
        g0 = 2 * RET_QK_WIDTH + RET_V_WIDTH + h * RET_V_DIM
        g = p_ref[0, :, g0:g0 + RET_V_DIM].astype(F32)
        q = _rope(jnp.broadcast_to(q, (8, RET_QK_DIM)), cos, sn, sp)
        k = _rope(jnp.broadcast_to(k, (8, RET_QK_DIM)), cos, sn, sp) * (RET_QK_DIM ** -0.5)
        qb = q.astype(BF16)
        kb = k.astype(BF16)
        vb = v.astype(BF16)
        s_old = st_ref[0, h]
        qk = jnp.sum(qb.astype(F32) * kb.astype(F32), axis=-1, keepdims=True)
        out = (qk.astype(BF16).astype(F32) * vb.astype(F32)
               + _dot((q * gamma).astype(BF16), s_old.astype(BF16)))
        k_row0 = jnp.where(row0, kb, jnp.zeros_like(kb))
        v8 = jnp.broadcast_to(vb, (8, RET_V_DIM))
        ns_ref[0, h] = gamma * s_old + _dot_tn(k_row0, v8)
        out = out[0:1]
        out = out * lax.rsqrt(jnp.mean(out * out, axis=-1, keepdims=True) + NORM_EPS)
        o_ref[0, :, h * RET_V_DIM:(h + 1) * RET_V_DIM] = (g * jax.nn.sigmoid(g) * out).astype(o_ref.dtype)


def _retention_sample(proj, state, pos):
    b = proj.shape[0]
    cos, sn, sp = _rope_tables(jnp.full((1,), pos, jnp.int32))
    rope = jnp.zeros((8, RET_QK_DIM), F32).at[0].set(cos[0]).at[1].set(sn[0]).at[2].set(sp[0])
    gam = jnp.broadcast_to(jnp.exp(_log_gamma())[:, None], (RET_HEADS, LANES))
    width = proj.shape[1]
    out, ns = pl.pallas_call(
        _ret_sample_body,
        grid=(b,),
        in_specs=[
            pl.BlockSpec((1, 1, width), lambda i: (i, 0, 0)),
            pl.BlockSpec((1, RET_HEADS, RET_QK_DIM, RET_V_DIM), lambda i: (i, 0, 0, 0)),
            pl.BlockSpec((8, RET_QK_DIM), lambda i: (0, 0)),
            pl.BlockSpec((RET_HEADS, LANES), lambda i: (0, 0)),
        ],
        out_specs=[
            pl.BlockSpec((1, 1, RET_V_WIDTH), lambda i: (i, 0, 0)),
            pl.BlockSpec((1, RET_HEADS, RET_QK_DIM, RET_V_DIM), lambda i: (i, 0, 0, 0)),
        ],
        out_shape=[jax.ShapeDtypeStruct((b, 1, RET_V_WIDTH), BF16),
                   jax.ShapeDtypeStruct(state.shape, state.dtype)],
        compiler_params=_cparams(("parallel",)),
        name="retention_sample",
    )(proj.reshape(b, 1, width), state, rope, gam)
    return out.reshape(b, RET_V_WIDTH), ns


FOX_TQ = 512


def _fox_prompt_body(qi_ref, ki_ref, q_ref, k_ref, v_ref, fk_ref, o_ref, m_s, l_s, acc_s):
    p = pl.program_id(2)
    qi, ki = qi_ref[p], ki_ref[p]

    @pl.when(ki == 0)
    def _():
        m_s[...] = jnp.full_like(m_s, MASK_VALUE)
        l_s[...] = jnp.zeros_like(l_s)
        acc_s[...] = jnp.zeros_like(acc_s)

    def step(masked):
        q2 = q_ref[0].astype(F32)
        k2 = k_ref[0]
        v2 = v_ref[0]
        lane = lax.broadcasted_iota(jnp.int32, q2.shape, 1)
        if masked:
            row = lax.broadcasted_iota(jnp.int32, (FOX_TQ, FOX_TQ), 0)
            col = lax.broadcasted_iota(jnp.int32, (FOX_TQ, FOX_TQ), 1)
            keep = col <= row
        for hh in range(2):
            in_head = (lane >= hh * FOX_HEAD_DIM) & (lane < (hh + 1) * FOX_HEAD_DIM)
            qm = jnp.where(in_head, q2 * (FOX_HEAD_DIM ** -0.5), 0.0).astype(BF16)
            s = _dot_nt(qm, k2) - fk_ref[0, hh]
            if masked:
                s = jnp.where(keep, s, MASK_VALUE)
            m_prev = m_s[hh]
            m_new = jnp.maximum(m_prev, jnp.max(s, axis=-1, keepdims=True))
            alpha = jnp.exp(m_prev - m_new)
            pr = jnp.exp(s - m_new)
            l_s[hh] = alpha * l_s[hh] + jnp.sum(pr, axis=-1, keepdims=True)
            acc_s[hh] = alpha * acc_s[hh] + _dot(pr.astype(BF16), v2)
            m_s[hh] = m_new

    @pl.when(ki < qi)
    def _():
        step(False)

    @pl.when(ki == qi)
    def _():
        step(True)
        lane = lax.broadcasted_iota(jnp.int32, (FOX_TQ, LANES), 1)
        o0 = acc_s[0] / l_s[0]
        o1 = acc_s[1] / l_s[1]
        o_ref[0] = jnp.where(lane < FOX_HEAD_DIM, o0, o1).astype(o_ref.dtype)


def _fox_prompt(q, k, v, fcum_t, batch, seq):
    n_q = seq // FOX_TQ
    pairs = [(i, j) for i in range(n_q) for j in range(i + 1)]
    qi_tab = jnp.asarray([p[0] for p in pairs], jnp.int32)
    ki_tab = jnp.asarray([p[1] for p in pairs], jnp.int32)
    n_hp = FOX_HEADS // 2
    grid_spec = pltpu.PrefetchScalarGridSpec(
        num_scalar_prefetch=2,
        grid=(batch, n_hp, len(pairs)),
        in_specs=[
            pl.BlockSpec((1, FOX_TQ, LANES), lambda b, h, p, qi, ki: (b, qi[p], h)),
            pl.BlockSpec((1, FOX_TQ, LANES), lambda b, h, p, qi, ki: (b, ki[p], h)),
            pl.BlockSpec((1, FOX_TQ, LANES), lambda b, h, p, qi, ki: (b, ki[p], h)),
            pl.BlockSpec((1, 2, 1, FOX_TQ), lambda b, h, p, qi, ki: (b, h, 0, ki[p])),
        ],
        out_specs=pl.BlockSpec((1, FOX_TQ, LANES), lambda b, h, p, qi, ki: (b, qi[p], h)),
        scratch_shapes=[pltpu.VMEM((2, FOX_TQ, 1), F32), pltpu.VMEM((2, FOX_TQ, 1), F32),
                        pltpu.VMEM((2, FOX_TQ, LANES), F32)],
    )
    return pl.pallas_call(
        _fox_prompt_body,
        grid_spec=grid_spec,
        out_shape=jax.ShapeDtypeStruct((batch, seq, FOX_HEADS * FOX_HEAD_DIM), BF16),
        compiler_params=_cparams(("parallel", "parallel", "arbitrary")),
        name="fox_prompt",
    )(qi_tab, ki_tab, q, k, v, fcum_t)


def _fox_decode_body(q_ref, kn_ref, vn_ref, kp_ref, vp_ref, bias_ref, o_ref):
    width = FOX_HEADS * FOX_HEAD_DIM
    lane = lax.broadcasted_iota(jnp.int32, (FOX_HEADS, width), 1)
    row = lax.broadcasted_iota(jnp.int32, (FOX_HEADS, width), 0)
    own = (lane >= row * FOX_HEAD_DIM) & (lane < (row + 1) * FOX_HEAD_DIM)
    scale = FOX_HEAD_DIM ** -0.5
    q_row = jnp.broadcast_to(q_ref[0].astype(F32), (FOX_HEADS, width))
    q_blk = jnp.where(own, q_row, 0.0).astype(BF16)
    s_past = _dot_nt(q_blk, kp_ref[0]) * scale + bias_ref[0]
    knb = kn_ref[0].astype(BF16).astype(F32)
    s_new = jnp.sum(q_blk.astype(F32) * knb, axis=-1, keepdims=True) * scale
    m = jnp.maximum(jnp.max(s_past, axis=-1, keepdims=True), s_new)
    p_past = jnp.exp(s_past - m)
    p_new = jnp.exp(s_new - m)
    denom = jnp.sum(p_past, axis=-1, keepdims=True) + p_new
    p_past = p_past / denom
    p_new = p_new / denom
    vnb = vn_ref[0].astype(BF16).astype(F32)
    o_full = _dot(p_past.astype(BF16), vp_ref[0]) + p_new.astype(BF16).astype(F32) * vnb
    o_ref[0] = jnp.sum(jnp.where(own, o_full, 0.0), axis=0, keepdims=True).astype(o_ref.dtype)


def _fox_decode(q, k_new, v_new, k_past, v_past, bias_t):
    b, width = q.shape
    p_len = k_past.shape[1]
    vec = pl.BlockSpec((1, 1, width), lambda i: (i, 0, 0))
    out = pl.pallas_call(
        _fox_decode_body,
        grid=(b,),
        in_specs=[vec, vec, vec,
                  pl.BlockSpec((1, p_len, width), lambda i: (i, 0, 0)),
                  pl.BlockSpec((1, p_len, width), lambda i: (i, 0, 0)),
                  pl.BlockSpec((1, FOX_HEADS, p_len), lambda i: (i, 0, 0))],
        out_specs=vec,
        out_shape=jax.ShapeDtypeStruct((b, 1, width), BF16),
        compiler_params=_cparams(("parallel",)),
        name="fox_decode",
    )(q.reshape(b, 1, width), k_new.reshape(b, 1, width), v_new.reshape(b, 1, width), k_past, v_past, bias_t)
    return out.reshape(b, width)


def kernel(x_prompt, x_sample, state_ret, cache_k, cache_v, cache_logf, page_table, c_prompt, c_sample,
           norm_mix, norm_ffn, ada_w, ada_b, ret_w_in, ret_w_out, fox_w_q, fox_w_o, norm_kv, ada_kv_w,
           ada_kv_b, kv_w, f_w, f_b, router_w, router_b, expert_w_gu, expert_b_gu, expert_w_down,
           expert_b_down, norm_final, ada_final_w, ada_final_b):
    d = D_MODEL
    bp, seq, _ = x_prompt.shape
    bs = x_sample.shape[0]
    tp = bp * seq
    past_len = page_table.shape[1] * cache_k.shape[1]

    ret_w_in_b = ret_w_in.astype(BF16)
    ret_w_out_b = ret_w_out.astype(BF16)
    fox_w_q_b = fox_w_q.astype(BF16)
    fox_w_o_b = fox_w_o.astype(BF16)
    kv_w_b = kv_w.astype(BF16)
    f_w_pad = jnp.zeros((d, LANES), BF16).at[:, :FOX_HEADS].set(f_w.astype(BF16))
    f_b_pad = jnp.zeros((LANES,), F32).at[:FOX_HEADS].set(f_b)
    router_w_pad = jnp.zeros((DEPTH, d, LANES), BF16).at[:, :, :N_EXPERTS].set(router_w.astype(BF16))
    router_b_pad = jnp.full((DEPTH, 1, LANES), -jnp.inf, F32).at[:, 0, :N_EXPERTS].set(router_b)
    w1 = jnp.concatenate([expert_w_gu[..., 0::2], expert_w_gu[..., 1::2]], axis=-1).astype(BF16)
    b1 = jnp.concatenate([expert_b_gu[..., 0::2], expert_b_gu[..., 1::2]], axis=-1)[:, :, None, :]
    w2 = expert_w_down.astype(BF16)
    b2 = expert_b_down[:, :, None, :]

    ada_all_w = jnp.concatenate([ada_w[l] for l in range(DEPTH)] + [ada_kv_w, ada_final_w], axis=1).astype(BF16)
    ada_all_b = jnp.concatenate([ada_b[l] for l in range(DEPTH)] + [ada_kv_b, ada_final_b], axis=0)
    n_c = bs + bp
    n_c_pad = ((n_c + 7) // 8) * 8
    c_all = jnp.zeros((n_c_pad, d), F32).at[:bs].set(c_sample).at[bs:n_c].set(c_prompt)
    mods = _linear(c_all, ada_all_w, pre_silu=True, bias=ada_all_b, tm=n_c_pad, tn=2048, name="ada_params")

    def mod_s(col):
        return mods[:bs, col * d:(col + 1) * d]

    def mod_p(col):
        return mods[bs:n_c, col * d:(col + 1) * d].reshape(bp, 1, d)

    xp = x_prompt.reshape(tp, d)
    xs = x_sample.reshape(bs, d)
    sample_pos = past_len

    ret_states_p, ret_states_s = [], []
    kv_p = kv_s = None
    for layer in range(DEPTH):
        base = layer * N_MOD
        gain_mix = norm_mix[layer]
        mix_mod_p = (gain_mix, mod_p(base + 0), mod_p(base + 1))
        mix_mod_s = (gain_mix, mod_s(base + 0), mod_s(base + 1))
        if layer < N_A_LAYERS:
            proj_p = _linear(xp, ret_w_in_b[layer], mod=mix_mod_p, out_dtype=BF16, tn=2048,
                             rows_per_group=seq, name="ret_in_prompt")
            gated_p, st_p = _retention_prompt(proj_p, bp, seq)
            xp = _linear(gated_p, ret_w_out_b[layer], epi=(xp, mod_p(base + 2)), rows_per_group=seq,
                         name="ret_out_prompt")
            ret_states_p.append(st_p)
            proj_s = _linear(xs, ret_w_in_b[layer], mod=mix_mod_s, out_dtype=F32, tn=2048, name="ret_in_sample")
            gated_s, st_s = _retention_sample(proj_s, state_ret[layer], sample_pos)
            xs = _linear(gated_s, ret_w_out_b[layer], epi=(xs, mod_s(base + 2)), name="ret_out_sample")
            ret_states_s.append(st_s)
        else:
            j = layer - N_A_LAYERS
            k_p, v_p, fcum_t, k_pb, v_pb = kv_p
            q_p = _linear(xp, fox_w_q_b[j], mod=mix_mod_p, out_dtype=BF16, rows_per_group=seq, name="fox_q_prompt")
            att_p = _fox_prompt(q_p.reshape(bp, seq, d), k_pb, v_pb, fcum_t, bp, seq)
            xp = _linear(att_p.reshape(tp, d), fox_w_o_b[j], epi=(xp, mod_p(base + 2)), rows_per_group=seq,
                         name="fox_o_prompt")
            k_s, v_s, k_past, v_past, bias_t = kv_s
            q_s = _linear(xs, fox_w_q_b[j], mod=mix_mod_s, out_dtype=F32, name="fox_q_sample")
            att_s = _fox_decode(q_s, k_s, v_s, k_past, v_past, bias_t)
            xs = _linear(att_s, fox_w_o_b[j], epi=(xs, mod_s(base + 2)), name="fox_o_sample")

        gain_ffn = norm_ffn[layer]
        h_p, ti_p, tw_p = _router(xp, gain_ffn, mod_p(base + 3), mod_p(base + 4), router_w_pad[layer],
                                  router_b_pad[layer], tm=512, rows_per_group=seq)
        h_s, ti_s, tw_s = _router(xs, gain_ffn, mod_s(base + 3), mod_s(base + 4), router_w_pad[layer],
                                  router_b_pad[layer], tm=bs)
        topi = jnp.concatenate([ti_p[:, :TOP_K], ti_s[:, :TOP_K]], axis=0)
        topw = jnp.concatenate([tw_p[:, :TOP_K], tw_s[:, :TOP_K]], axis=0)
        g2_p = jnp.broadcast_to(mod_p(base + 5), (bp, seq, d)).reshape(tp, d)
        xp, xs = _moe_layer([xp, xs], [h_p, h_s], topi, topw, [g2_p, mod_s(base + 5)],
                            w1[layer], b1[layer], w2[layer], b2[layer])

        if layer == N_A_LAYERS - 1:
            kvb = DEPTH * N_MOD
            kv_mod_p = (norm_kv, mod_p(kvb + 0), mod_p(kvb + 1))
            kv_mod_s = (norm_kv, mod_s(kvb + 0), mod_s(kvb + 1))
            kvp = _linear(xp, kv_w_b, mod=kv_mod_p, rows_per_group=seq, name="kv_prompt")
            lf_p = _linear(xp, f_w_pad, mod=kv_mod_p, bias=f_b_pad, post="log_sigmoid", rows_per_group=seq,
                           name="logf_prompt")[:, :FOX_HEADS]
            k_p, v_p = kvp[:, :d], kvp[:, d:]
            logf_p = lf_p.reshape(bp, seq, FOX_HEADS)
            fcum_t = jnp.transpose(jnp.cumsum(logf_p, axis=1), (0, 2, 1))[:, :, None, :]
            kv_p = (k_p, v_p, fcum_t, k_p.astype(BF16).reshape(bp, seq, d), v_p.astype(BF16).reshape(bp, seq, d))

            kvs = _linear(xs, kv_w_b, mod=kv_mod_s, name="kv_sample")
            lf_s = _linear(xs, f_w_pad, mod=kv_mod_s, bias=f_b_pad, post="log_sigmoid",
                           name="logf_sample")[:, :FOX_HEADS]
            k_s, v_s = kvs[:, :d], kvs[:, d:]
            k_past = cache_k[page_table].reshape(bs, past_len, d).astype(BF16)
            v_past = cache_v[page_table].reshape(bs, past_len, d).astype(BF16)
            lf_past = cache_logf[page_table].reshape(bs, past_len, FOX_HEADS).astype(F32)
            cs = jnp.cumsum(lf_past, axis=1)
            bias = (cs[:, -1:, :] - cs) + lf_s[:, None, :]
            kv_s = (k_s, v_s, k_past, v_past, jnp.transpose(bias, (0, 2, 1)))

    fb = DEPTH * N_MOD + 2
    y_p = _linear(xp, None, mod=(norm_final, mod_p(fb + 0), mod_p(fb + 1)), rows_per_group=seq, name="final_prompt")
    y_s = _linear(xs, None, mod=(norm_final, mod_s(fb + 0), mod_s(fb + 1)), name="final_sample")

    k_p, v_p = kv_p[0], kv_p[1]
    k_s, v_s = kv_s[0], kv_s[1]
    return (y_p.reshape(bp, seq, d),
            y_s.reshape(bs, 1, d),
            jnp.stack(ret_states_p),
            jnp.stack(ret_states_s),
            k_p.reshape(bp, seq, FOX_HEADS, FOX_HEAD_DIM),
            v_p.reshape(bp, seq, FOX_HEADS, FOX_HEAD_DIM),
            logf_p,
            k_s.reshape(bs, 1, FOX_HEADS, FOX_HEAD_DIM),
            v_s.reshape(bs, 1, FOX_HEADS, FOX_HEAD_DIM),
            lf_s.reshape(bs, 1, FOX_HEADS))
```

```python
import functools

import jax
import jax.numpy as jnp
import numpy as np
from jax import lax
from jax.experimental import pallas as pl
from jax.experimental.pallas import tpu as pltpu

F32 = jnp.float32
BF16 = jnp.bfloat16

D_MODEL = 1024
DEPTH = 4
N_A_LAYERS = 2
RET_HEADS = 4
RET_QK_DIM = 256
RET_V_DIM = 512
RET_QK_WIDTH = 1024
RET_V_WIDTH = 2048
RET_CHUNK = 128
ROPE_BASE = 10000.0
FOX_HEADS = 16
FOX_HEAD_DIM = 64
N_EXPERTS = 32
TOP_K = 4
D_FF = 1024
SWIGLU_LIMIT = 7.0
SWIGLU_ALPHA = 1.702
NORM_EPS = 1e-6
N_MOD = 6
MASK_VALUE = -1e30

V7X_VMEM_LIMIT = 56 * 1024 * 1024
LANES = 128


def _cparams(sem):
    return pltpu.CompilerParams(dimension_semantics=sem, vmem_limit_bytes=V7X_VMEM_LIMIT)


def _dot(a, b):
    return jnp.dot(a, b, preferred_element_type=F32)


def _dot_nt(a, b):
    return lax.dot_general(a, b, (((1,), (1,)), ((), ())), preferred_element_type=F32)


def _dot_tn(a, b):
    return lax.dot_general(a, b, (((0,), (0,)), ((), ())), preferred_element_type=F32)


def _modulated(x, gain, shift, scale):
    xf = x.astype(F32)
    y = xf * lax.rsqrt(jnp.mean(xf * xf, axis=-1, keepdims=True) + NORM_EPS)
    return (y * gain) * (1.0 + scale) + shift


def _log_sigmoid(z):
    return jnp.minimum(z, 0.0) - jnp.log1p(jnp.exp(-jnp.abs(z)))


def _linear_body(*refs, mod, pre_silu, has_bias, epi, post, n_j, only_mod):
    it = iter(refs)
    x_ref = next(it)
    if mod:
        gain_ref, shift_ref, scale_ref = next(it), next(it), next(it)
    if not only_mod:
        w_ref = next(it)
    b_ref = next(it) if has_bias else None
    if epi:
        res_ref, gate_ref = next(it), next(it)
    o_ref = next(it)
    h_ref = next(it) if n_j > 1 else None

    def prologue():
        x = x_ref[...]
        if mod:
            return _modulated(x, gain_ref[...], shift_ref[0], scale_ref[0])
        if pre_silu:
            xf = x.astype(F32)
            return xf * jax.nn.sigmoid(xf)
        return x

    if only_mod:
        o_ref[...] = prologue().astype(o_ref.dtype)
        return

    if n_j > 1:
        @pl.when(pl.program_id(1) == 0)
        def _():
            h_ref[...] = prologue().astype(BF16)
        h = h_ref[...]
    else:
        h = prologue().astype(BF16)

    acc = _dot(h, w_ref[...])
    if has_bias:
        acc = acc + b_ref[...]
    if post == "log_sigmoid":
        acc = _log_sigmoid(acc)
    if epi:
        acc = res_ref[...] + gate_ref[0] * acc
    o_ref[...] = acc.astype(o_ref.dtype)


def _linear(x, w=None, *, mod=None, pre_silu=False, bias=None, epi=None, post=None,
            out_dtype=F32, tm=512, tn=1024, rows_per_group=None, name="linear"):
    m, k = x.shape
    only_mod = w is None
    n = k if only_mod else w.shape[1]
    tm = min(tm, m)
    tn = n if only_mod else min(tn, n)
    assert m % tm == 0 and n % tn == 0
    n_i, n_j = m // tm, n // tn
    if only_mod:
        n_j = 1

    def row_or_group(arr, width, col_tiled):
        if arr.ndim == 3:
            tiles_per_group = rows_per_group // tm
            assert rows_per_group % tm == 0
            if col_tiled:
                return arr, pl.BlockSpec((1, 1, tn), lambda i, j: (i // tiles_per_group, 0, j))
            return arr, pl.BlockSpec((1, 1, width), lambda i, j: (i // tiles_per_group, 0, 0))
        arr3 = arr.reshape(n_i, tm, width)
        if col_tiled:
            return arr3, pl.BlockSpec((1, tm, tn), lambda i, j: (i, 0, j))
        return arr3, pl.BlockSpec((1, tm, width), lambda i, j: (i, 0, 0))

    args = [x]
    specs = [pl.BlockSpec((tm, k), lambda i, j: (i, 0))]
    if mod is not None:
        gain, shift, scale = mod
        args.append(gain.reshape(1, k).astype(F32))
        specs.append(pl.BlockSpec((1, k), lambda i, j: (0, 0)))
        for a in (shift, scale):
            a3, sp = row_or_group(a, k, False)
            args.append(a3)
            specs.append(sp)
    if not only_mod:
        args.append(w)
        specs.append(pl.BlockSpec((k, tn), lambda i, j: (0, j)))
    if bias is not None:
        args.append(bias.reshape(1, n).astype(F32))
        specs.append(pl.BlockSpec((1, tn), lambda i, j: (0, j)))
    if epi is not None:
        res, gate = epi
        args.append(res)
        specs.append(pl.BlockSpec((tm, tn), lambda i, j: (i, j)))
        g3, sp = row_or_group(gate, n, True)
        args.append(g3)
        specs.append(sp)

    body = functools.partial(_linear_body, mod=mod is not None, pre_silu=pre_silu,
                             has_bias=bias is not None, epi=epi is not None, post=post,
                             n_j=n_j, only_mod=only_mod)
    scratch = [pltpu.VMEM((tm, k), BF16)] if n_j > 1 else []
    return pl.pallas_call(
        body,
        grid=(n_i, n_j),
        in_specs=specs,
        out_specs=pl.BlockSpec((tm, tn), lambda i, j: (i, j)),
        out_shape=jax.ShapeDtypeStruct((m, n), out_dtype),
        scratch_shapes=scratch,
        compiler_params=_cparams(("parallel", "arbitrary")),
        name=name,
    )(*args)


def _router_body(x_ref, gain_ref, shift_ref, scale_ref, wr_ref, br_ref, h_ref, ti_ref, tw_ref):
    h = _modulated(x_ref[...], gain_ref[...], shift_ref[0], scale_ref[0]).astype(BF16)
    h_ref[...] = h
    logits = _dot(h, wr_ref[...]) + br_ref[...]
    lane = lax.broadcasted_iota(jnp.int32, logits.shape, 1).astype(F32)
    vals, idxs = [], []
    cur = logits
    for _ in range(TOP_K):
        mx = jnp.max(cur, axis=-1, keepdims=True)
        idx = jnp.min(jnp.where(cur == mx, lane, float(LANES)), axis=-1, keepdims=True)
        vals.append(mx)
        idxs.append(idx)
        cur = jnp.where(lane == idx, -jnp.inf, cur)
    exps = [jnp.exp(v - vals[0]) for v in vals]
    denom = exps[0] + exps[1] + exps[2] + exps[3]
    ti = jnp.zeros(logits.shape, F32)
    tw = jnp.zeros(logits.shape, F32)
    for kk in range(TOP_K):
        ti = jnp.where(lane == float(kk), idxs[kk], ti)
        tw = jnp.where(lane == float(kk), exps[kk] / denom, tw)
    ti_ref[...] = ti.astype(jnp.int32)
    tw_ref[...] = tw


def _router(x, gain, shift, scale, wr_pad, br_pad, *, tm, rows_per_group=None):
    m, k = x.shape
    tm = min(tm, m)
    n_i = m // tm
    if shift.ndim == 3:
        tpg = rows_per_group // tm
        mod_spec = pl.BlockSpec((1, 1, k), lambda i: (i // tpg, 0, 0))
        sh3, sc3 = shift, scale
    else:
        mod_spec = pl.BlockSpec((1, tm, k), lambda i: (i, 0, 0))
        sh3, sc3 = shift.reshape(n_i, tm, k), scale.reshape(n_i, tm, k)
    return pl.pallas_call(
        _router_body,
        grid=(n_i,),
        in_specs=[pl.BlockSpec((tm, k), lambda i: (i, 0)),
                  pl.BlockSpec((1, k), lambda i: (0, 0)),
                  mod_spec, mod_spec,
                  pl.BlockSpec((k, LANES), lambda i: (0, 0)),
                  pl.BlockSpec((1, LANES), lambda i: (0, 0))],
        out_specs=[pl.BlockSpec((tm, k), lambda i: (i, 0)),
                   pl.BlockSpec((tm, LANES), lambda i: (i, 0)),
                   pl.BlockSpec((tm, LANES), lambda i: (i, 0))],
        out_shape=[jax.ShapeDtypeStruct((m, k), BF16),
                   jax.ShapeDtypeStruct((m, LANES), jnp.int32),
                   jax.ShapeDtypeStruct((m, LANES), F32)],
        compiler_params=_cparams(("parallel",)),
        name="moe_router",
    )(x, gain.reshape(1, k).astype(F32), sh3, sc3, wr_pad, br_pad)


MOE_TILE = 512
GU_GROUP = 2 * LANES


def _deint_body(w_ref, p_ref, o_ref):
    for c in range(w_ref.shape[2] // GU_GROUP):
        cols = slice(c * GU_GROUP, (c + 1) * GU_GROUP)
        o_ref[0, :, cols] = _dot(w_ref[0, :, cols].astype(BF16), p_ref[...]).astype(o_ref.dtype)


def _deinterleave_gu(w, rows=1024):
    e, k, n = w.shape
    src = jnp.arange(GU_GROUP)
    dst = (src % 2) * LANES + src // 2
    perm = jnp.zeros((GU_GROUP, GU_GROUP), BF16).at[src, dst].set(1.0)
    return pl.pallas_call(
        _deint_body,
        grid=(e, k // rows),
        in_specs=[pl.BlockSpec((1, rows, n), lambda i, j: (i, j, 0)),
                  pl.BlockSpec((GU_GROUP, GU_GROUP), lambda i, j: (0, 0))],
        out_specs=pl.BlockSpec((1, rows, n), lambda i, j: (i, j, 0)),
        out_shape=jax.ShapeDtypeStruct((e, k, n), BF16),
        compiler_params=_cparams(("parallel", "parallel")),
        name="moe_weight_regroup",
    )(w, perm)


def _regroup_gu_bias(b):
    lead = b.shape[:-1]
    g = b.reshape(*lead, -1, LANES, 2)
    return jnp.swapaxes(g, -1, -2).reshape(*lead, -1)


def _gmm_body(te_ref, tv_ref, hs_ref, w1_ref, b1_ref, w2_ref, b2_ref, o_ref):
    i = pl.program_id(0)

    @pl.when(tv_ref[i] > 0)
    def _():
        gu = _dot(hs_ref[...], w1_ref[0]) + b1_ref[0]
        acts = []
        for c in range(2 * D_FF // GU_GROUP):
            glu = jnp.minimum(gu[:, c * GU_GROUP:c * GU_GROUP + LANES], SWIGLU_LIMIT)
            lin = jnp.clip(gu[:, c * GU_GROUP + LANES:(c + 1) * GU_GROUP], -SWIGLU_LIMIT, SWIGLU_LIMIT)
            acts.append((glu * jax.nn.sigmoid(SWIGLU_ALPHA * glu) * (lin + 1.0)).astype(BF16))
        act = jnp.concatenate(acts, axis=1)
        y = _dot(act, w2_ref[0]) + b2_ref[0]
        o_ref[...] = y.astype(o_ref.dtype)


def _moe_gmm(tile_expert, tile_valid, tile_block, hs, w1, b1, w2, b2):
    p_rows, d = hs.shape
    n_tiles = p_rows // MOE_TILE
    grid_spec = pltpu.PrefetchScalarGridSpec(
        num_scalar_prefetch=3,
        grid=(n_tiles,),
        in_specs=[
            pl.BlockSpec((MOE_TILE, d), lambda i, te, tv, tb: (tb[i], 0)),
            pl.BlockSpec((1, d, 2 * D_FF), lambda i, te, tv, tb: (te[i], 0, 0)),
            pl.BlockSpec((1, 1, 2 * D_FF), lambda i, te, tv, tb: (te[i], 0, 0)),
            pl.BlockSpec((1, D_FF, d), lambda i, te, tv, tb: (te[i], 0, 0)),
            pl.BlockSpec((1, 1, d), lambda i, te, tv, tb: (te[i], 0, 0)),
        ],
        out_specs=pl.BlockSpec((MOE_TILE, d), lambda i, te, tv, tb: (tb[i], 0)),
    )

    def body(te_ref, tv_ref, tb_ref, *rest):
        _gmm_body(te_ref, tv_ref, *rest)

    return pl.pallas_call(
        body,
        grid_spec=grid_spec,
        out_shape=jax.ShapeDtypeStruct((p_rows, d), BF16),
        compiler_params=_cparams(("arbitrary",)),
        name="moe_gmm",
    )(tile_expert, tile_valid, tile_block, hs, w1, b1, w2, b2)


def _moe_layer(x_parts, h_parts, topi, topw, gates2, w1, b1, w2, b2):
    h_all = jnp.concatenate(h_parts, axis=0)
    t = h_all.shape[0]
    n_slots = t * TOP_K
    flat_e = topi.reshape(n_slots)
    onehot = (flat_e[:, None] == jnp.arange(N_EXPERTS, dtype=jnp.int32)[None, :]).astype(jnp.int32)
    csum = jnp.cumsum(onehot, axis=0)
    counts = csum[-1]
    rank = jnp.sum(csum * onehot, axis=1) - 1
    padded = ((counts + MOE_TILE - 1) // MOE_TILE) * MOE_TILE
    poffs = jnp.cumsum(padded) - padded
    total_padded = jnp.sum(padded)
    p_rows = ((n_slots + N_EXPERTS * (MOE_TILE - 1)) // MOE_TILE + 1) * MOE_TILE
    n_tiles = p_rows // MOE_TILE
    tile_start = jnp.arange(n_tiles, dtype=jnp.int32) * MOE_TILE
    tile_valid = (tile_start < total_padded).astype(jnp.int32)
    pends = poffs + padded
    tile_expert = jnp.minimum(jnp.sum((pends[None, :] <= tile_start[:, None]).astype(jnp.int32), axis=1),
                              N_EXPERTS - 1)
    last_valid = jnp.maximum(total_padded // MOE_TILE - 1, 0).astype(jnp.int32)
    tile_block = jnp.minimum(jnp.arange(n_tiles, dtype=jnp.int32), last_valid)
    tile_expert = jnp.where(tile_valid > 0, tile_expert, tile_expert[last_valid])
    inv = poffs[flat_e] + rank
    row_token = jnp.zeros((p_rows,), jnp.int32).at[inv].set(jnp.arange(n_slots, dtype=jnp.int32) // TOP_K)

    hs = jnp.take(h_all, row_token, axis=0)
    ys = _moe_gmm(tile_expert, tile_valid, tile_block, hs, w1, b1, w2, b2)
    yk = jnp.take(ys, inv, axis=0).reshape(t, TOP_K, -1).astype(F32)
    moe = jnp.sum(yk * topw[:, :, None], axis=1)
    outs, start = [], 0
    for xp, g2 in zip(x_parts, gates2):
        r = xp.shape[0]
        outs.append(xp + g2 * moe[start:start + r])
        start += r
    return outs


def _rope_tables(pos):
    half = RET_QK_DIM // 2
    inv = 1.0 / (ROPE_BASE ** jnp.linspace(0.0, 1.0, half, dtype=F32))
    ang = jnp.repeat(pos.astype(F32)[:, None] * inv[None, :], 2, axis=-1)
    cos, sin = jnp.cos(ang), jnp.sin(ang)
    even = (jnp.arange(RET_QK_DIM) % 2 == 0)[None, :]
    return cos, jnp.where(even, -sin, 0.0), jnp.where(even, 0.0, sin)


def _rope(x, cos, sin_next, sin_prev):
    n = x.shape[-1]
    return x * cos + pltpu.roll(x, n - 1, 1) * sin_next + pltpu.roll(x, 1, 1) * sin_prev


def _log_gamma():
    return jnp.log1p(-jnp.exp2(-5.0 - jnp.arange(RET_HEADS, dtype=F32)))


def _ret_prompt_body(q_ref, k_ref, v_ref, g_ref, cos_ref, sn_ref, sp_ref, dec_ref, rd_ref,
                     o_ref, st_ref, state):
    c = pl.program_id(2)

    @pl.when(c == 0)
    def _():
        state[...] = jnp.zeros_like(state)

    cos, sn, sp = cos_ref[...], sn_ref[...], sp_ref[...]
    q = _rope(q_ref[0].astype(F32), cos, sn, sp)
    k = _rope(k_ref[0].astype(F32), cos, sn, sp) * (RET_QK_DIM ** -0.5)
    v = v_ref[0]
    rd = rd_ref[0]
    q_dec = (q * rd[:, 0:1]).astype(BF16)
    k_dec = (k * rd[:, 1:2]).astype(BF16)
    inner = (_dot_nt(q.astype(BF16), k.astype(BF16)) * dec_ref[0]).astype(BF16)
    s_old = state[...]
    out = _dot(inner, v) + _dot(q_dec, s_old.astype(BF16))
    state[...] = rd[0:1, 2:3] * s_old + _dot_tn(k_dec, v)
    out = out * lax.rsqrt(jnp.mean(out * out, axis=-1, keepdims=True) + NORM_EPS)
    g = g_ref[0].astype(F32)
    o_ref[0] = (g * jax.nn.sigmoid(g) * out).astype(o_ref.dtype)

    @pl.when(c == pl.num_programs(2) - 1)
    def _():
        st_ref[0, 0] = state[...]


def _retention_prompt(proj, batch, seq):
    c = RET_CHUNK
    n_chunks = seq // c
    proj3 = proj.reshape(batch, seq, -1)
    cos, sn, sp = _rope_tables(jnp.arange(seq, dtype=jnp.int32))
    lg = _log_gamma()
    idx = jnp.arange(c, dtype=F32)
    diff = idx[:, None] - idx[None, :]
    decay = jnp.where(diff >= 0, jnp.exp(jnp.maximum(diff, 0.0)[None] * lg[:, None, None]), 0.0)
    qd = jnp.exp((idx + 1.0)[None, :] * lg[:, None])
    kd = jnp.exp((c - 1.0 - idx)[None, :] * lg[:, None])
    cd = jnp.broadcast_to(jnp.exp(c * lg)[:, None], (RET_HEADS, c))
    rd = jnp.zeros((RET_HEADS, c, LANES), F32).at[:, :, 0].set(qd).at[:, :, 1].set(kd).at[:, :, 2].set(cd)
    nqk = RET_QK_WIDTH // RET_QK_DIM
    out, st = pl.pallas_call(
        _ret_prompt_body,
        grid=(batch, RET_HEADS, n_chunks),
        in_specs=[
            pl.BlockSpec((1, c, RET_QK_DIM), lambda b, h, i: (b, i, h)),
            pl.BlockSpec((1, c, RET_QK_DIM), lambda b, h, i: (b, i, nqk + h)),
            pl.BlockSpec((1, c, RET_V_DIM), lambda b, h, i: (b, i, nqk + h)),
            pl.BlockSpec((1, c, RET_V_DIM), lambda b, h, i: (b, i, 2 * nqk + h)),
            pl.BlockSpec((c, RET_QK_DIM), lambda b, h, i: (i, 0)),
            pl.BlockSpec((c, RET_QK_DIM), lambda b, h, i: (i, 0)),
            pl.BlockSpec((c, RET_QK_DIM), lambda b, h, i: (i, 0)),
            pl.BlockSpec((1, c, c), lambda b, h, i: (h, 0, 0)),
            pl.BlockSpec((1, c, LANES), lambda b, h, i: (h, 0, 0)),
        ],
        out_specs=[
            pl.BlockSpec((1, c, RET_V_DIM), lambda b, h, i: (b, i, h)),
            pl.BlockSpec((1, 1, RET_QK_DIM, RET_V_DIM), lambda b, h, i: (b, h, 0, 0)),
        ],
        out_shape=[jax.ShapeDtypeStruct((batch, seq, RET_V_WIDTH), BF16),
                   jax.ShapeDtypeStruct((batch, RET_HEADS, RET_QK_DIM, RET_V_DIM), F32)],
        scratch_shapes=[pltpu.VMEM((RET_QK_DIM, RET_V_DIM), F32)],
        compiler_params=_cparams(("parallel", "parallel", "arbitrary")),
        name="retention_prompt",
    )(proj3, proj3, proj3, proj3, cos, sn, sp, decay, rd)
    return out.reshape(batch * seq, RET_V_WIDTH), st


def _ret_sample_body(p_ref, st_ref, rope_ref, gam_ref, o_ref, ns_ref):
    cos, sn, sp = rope_ref[0:1, :], rope_ref[1:2, :], rope_ref[2:3, :]
    row0 = lax.broadcasted_iota(jnp.int32, (8, RET_QK_DIM), 0) == 0
    for h in range(RET_HEADS):
        gamma = gam_ref[h:h + 1, 0:1]
        q = p_ref[0, :, h * RET_QK_DIM:(h + 1) * RET_QK_DIM].astype(F32)
        k = p_ref[0, :, RET_QK_WIDTH + h * RET_QK_DIM:RET_QK_WIDTH + (h + 1) * RET_QK_DIM].astype(F32)
        v0 = 2 * RET_QK_WIDTH + h * RET_V_DIM
        v = p_ref[0, :, v0:v0 + RET_V_DIM].astype(F32)
        g0 = 2 * RET_QK_WIDTH + RET_V_WIDTH + h * RET_V_DIM
        g = p_ref[0, :, g0:g0 + RET_V_DIM].astype(F32)
        q = _rope(jnp.broadcast_to(q, (8, RET_QK_DIM)), cos, sn, sp)
        k = _rope(jnp.broadcast_to(k, (8, RET_QK_DIM)), cos, sn, sp) * (RET_QK_DIM ** -0.5)
        qb = q.astype(BF16)
        kb = k.astype(BF16)
        vb = v.astype(BF16)
        s_old = st_ref[0, h]
        qk = jnp.sum(qb.astype(F32) * kb.astype(F32), axis=-1, keepdims=True)
        out = (qk.astype(BF16).astype(F32) * vb.astype(F32)
               + _dot((q * gamma).astype(BF16), s_old.astype(BF16)))
        k_row0 = jnp.where(row0, kb, jnp.zeros_like(kb))
        v8 = jnp.broadcast_to(vb, (8, RET_V_DIM))
        ns_ref[0, h] = gamma * s_old + _dot_tn(k_row0, v8)
        out = out[0:1]
        out = out * lax.rsqrt(jnp.mean(out * out, axis=-1, keepdims=True) + NORM_EPS)
        o_ref[0, :, h * RET_V_DIM:(h + 1) * RET_V_DIM] = (g * jax.nn.sigmoid(g) * out).astype(o_ref.dtype)


def _retention_sample(proj, state, pos):
    b = proj.shape[0]
    cos, sn, sp = _rope_tables(jnp.full((1,), pos, jnp.int32))
    rope = jnp.zeros((8, RET_QK_DIM), F32).at[0].set(cos[0]).at[1].set(sn[0]).at[2].set(sp[0])
    gam = jnp.broadcast_to(jnp.exp(_log_gamma())[:, None], (RET_HEADS, LANES))
    width = proj.shape[1]
    out, ns = pl.pallas_call(
        _ret_sample_body,
        grid=(b,),
        in_specs=[
            pl.BlockSpec((1, 1, width), lambda i: (i, 0, 0)),
            pl.BlockSpec((1, RET_HEADS, RET_QK_DIM, RET_V_DIM), lambda i: (i, 0, 0, 0)),
            pl.BlockSpec((8, RET_QK_DIM), lambda i: (0, 0)),
            pl.BlockSpec((RET_HEADS, LANES), lambda i: (0, 0)),
        ],
        out_specs=[
            pl.BlockSpec((1, 1, RET_V_WIDTH), lambda i: (i, 0, 0)),
            pl.BlockSpec((1, RET_HEADS, RET_QK_DIM, RET_V_DIM), lambda i: (i, 0, 0, 0)),
        ],
        out_shape=[jax.ShapeDtypeStruct((b, 1, RET_V_WIDTH), BF16),
                   jax.ShapeDtypeStruct(state.shape, state.dtype)],
        compiler_params=_cparams(("parallel",)),
        name="retention_sample",
    )(proj.reshape(b, 1, width), state, rope, gam)
    return out.reshape(b, RET_V_WIDTH), ns


FOX_TQ = 1024
FOX_BIAS_PARTS = 3


def _fox_bias_rows(fcum):
    b, s, h = fcum.shape
    neg = -fcum
    hi = neg.astype(BF16)
    r1 = neg - hi.astype(F32)
    mid = r1.astype(BF16)
    lo = (r1 - mid.astype(F32)).astype(BF16)
    parts = jnp.stack([hi, mid, lo], axis=-1).reshape(b, s, h // 2, 2 * FOX_BIAS_PARTS)
    rows = jnp.zeros((b, s, h // 2, LANES), BF16).at[..., :2 * FOX_BIAS_PARTS].set(parts)
    return jnp.transpose(rows, (0, 2, 1, 3))


def _fox_prompt_body(qi_ref, ki_ref, q_ref, k_ref, v_ref, kb_ref, o_ref, qa_s, m_s, acc_s):
    p = pl.program_id(2)
    qi, ki = qi_ref[p], ki_ref[p]
    tq = FOX_TQ
    lane = lax.broadcasted_iota(jnp.int32, (tq, LANES), 1)

    def head_lanes(hh):
        return (lane >= hh * FOX_HEAD_DIM) & (lane < (hh + 1) * FOX_HEAD_DIM)

    @pl.when(ki == 0)
    def _():
        q2 = q_ref[0].astype(F32) * (FOX_HEAD_DIM ** -0.5)
        for hh in range(2):
            qm = jnp.where(head_lanes(hh), q2, 0.0)
            sel = jnp.where((lane >= FOX_BIAS_PARTS * hh) & (lane < FOX_BIAS_PARTS * (hh + 1)), 1.0, 0.0)
            qa_s[hh] = jnp.concatenate([qm, sel], axis=1).astype(BF16)
        m_s[...] = jnp.full_like(m_s, MASK_VALUE)
        acc_s[...] = jnp.zeros_like(acc_s)

    def step(masked):
        k_aug = jnp.concatenate([k_ref[0], kb_ref[0, 0]], axis=1)
        v2 = v_ref[0].astype(F32)
        if masked:
            row = lax.broadcasted_iota(jnp.int32, (tq, tq), 0)
            col = lax.broadcasted_iota(jnp.int32, (tq, tq), 1)
            keep = col <= row
        for hh in range(2):
            v_aug = jnp.where(head_lanes(hh), v2, 1.0).astype(BF16)
            s = _dot_nt(qa_s[hh], k_aug)
            if masked:
                s = jnp.where(keep, s, MASK_VALUE)
            m_prev = m_s[hh]
            m_next = jnp.maximum(m_prev, jnp.max(s, axis=1, keepdims=True))
            alpha = jnp.exp(m_prev - m_next)
            pr = jnp.exp(s - jnp.tile(m_next, (1, tq // LANES)))
            acc_s[hh] = alpha * acc_s[hh] + _dot(pr.astype(BF16), v_aug)
            m_s[hh] = m_next

    @pl.when(ki < qi)
    def _():
        step(False)

    @pl.when(ki == qi)
    def _():
        step(True)
        a0, a1 = acc_s[0], acc_s[1]
        o0 = a0 / pltpu.roll(a0, FOX_HEAD_DIM, 1)
        o1 = a1 / pltpu.roll(a1, FOX_HEAD_DIM, 1)
        o_ref[0] = jnp.where(lane < FOX_HEAD_DIM, o0, o1).astype(o_ref.dtype)


def _fox_prompt(q, k, v, bias_rows, batch, seq):
    tq = FOX_TQ
    n_q = seq // tq
    pairs = [(i, j) for i in range(n_q) for j in range(i + 1)]
    qi_tab = jnp.asarray([p[0] for p in pairs], jnp.int32)
    ki_tab = jnp.asarray([p[1] for p in pairs], jnp.int32)
    n_hp = FOX_HEADS // 2
    grid_spec = pltpu.PrefetchScalarGridSpec(
        num_scalar_prefetch=2,
        grid=(batch, n_hp, len(pairs)),
        in_specs=[
            pl.BlockSpec((1, tq, LANES), lambda b, h, p, qi, ki: (b, qi[p], h)),
            pl.BlockSpec((1, tq, LANES), lambda b, h, p, qi, ki: (b, ki[p], h)),
            pl.BlockSpec((1, tq, LANES), lambda b, h, p, qi, ki: (b, ki[p], h)),
            pl.BlockSpec((1, 1, tq, LANES), lambda b, h, p, qi, ki: (b, h, ki[p], 0)),
        ],
        out_specs=pl.BlockSpec((1, tq, LANES), lambda b, h, p, qi, ki: (b, qi[p], h)),
        scratch_shapes=[pltpu.VMEM((2, tq, 2 * LANES), BF16), pltpu.VMEM((2, tq, LANES), F32),
                        pltpu.VMEM((2, tq, LANES), F32)],
    )
    return pl.pallas_call(
        _fox_prompt_body,
        grid_spec=grid_spec,
        out_shape=jax.ShapeDtypeStruct((batch, seq, FOX_HEADS * FOX_HEAD_DIM), BF16),
        compiler_params=_cparams(("parallel", "parallel", "arbitrary")),
        name="fox_prompt",
    )(qi_tab, ki_tab, q, k, v, bias_rows)


def _compact_body(pt_ref, k_ref, v_ref, ok_ref, ov_ref):
    for src, dst in ((k_ref, ok_ref), (v_ref, ov_ref)):
        page, heads, hd = src.shape[1:]
        dst[0] = src[0].reshape(page, heads * hd).astype(dst.dtype)


def _compact_pages(page_table, cache_k, cache_v):
    b, n_pages = page_table.shape
    _, page, heads, hd = cache_k.shape
    width = heads * hd
    in_spec = pl.BlockSpec((1, page, heads, hd), lambda i, p, pt: (pt[i, p], 0, 0, 0))
    out_spec = pl.BlockSpec((1, page, width), lambda i, p, pt: (i, p, 0))
    grid_spec = pltpu.PrefetchScalarGridSpec(
        num_scalar_prefetch=1,
        grid=(b, n_pages),
        in_specs=[in_spec, in_spec],
        out_specs=[out_spec, out_spec],
    )
    shape = jax.ShapeDtypeStruct((b, n_pages * page, width), BF16)
    return pl.pallas_call(
        _compact_body,
        grid_spec=grid_spec,
        out_shape=[shape, shape],
        compiler_params=_cparams(("parallel", "arbitrary")),
        name="kv_page_gather",
    )(page_table, cache_k, cache_v)


def _fox_decode_body(q_ref, kn_ref, vn_ref, kp_ref, vp_ref, bias_ref, o_ref):
    width = FOX_HEADS * FOX_HEAD_DIM
    lane = lax.broadcasted_iota(jnp.int32, (FOX_HEADS, width), 1)
    row = lax.broadcasted_iota(jnp.int32, (FOX_HEADS, width), 0)
    own = (lane >= row * FOX_HEAD_DIM) & (lane < (row + 1) * FOX_HEAD_DIM)
    scale = FOX_HEAD_DIM ** -0.5
    q_row = jnp.broadcast_to(q_ref[0].astype(F32), (FOX_HEADS, width))
    q_blk = jnp.where(own, q_row, 0.0).astype(BF16)
    s_past = _dot_nt(q_blk, kp_ref[0]) * scale + bias_ref[0]
    knb = kn_ref[0].astype(BF16).astype(F32)
    s_new = jnp.sum(q_blk.astype(F32) * knb, axis=-1, keepdims=True) * scale
    m = jnp.maximum(jnp.max(s_past, axis=-1, keepdims=True), s_new)
    p_past = jnp.exp(s_past - m)
    p_new = jnp.exp(s_new - m)
    denom = jnp.sum(p_past, axis=-1, keepdims=True) + p_new
    p_past = p_past / denom
    p_new = p_new / denom
    vnb = vn_ref[0].astype(BF16).astype(F32)
    o_full = _dot(p_past.astype(BF16), vp_ref[0]) + p_new.astype(BF16).astype(F32) * vnb
    o_ref[0] = jnp.sum(jnp.where(own, o_full, 0.0), axis=0, keepdims=True).astype(o_ref.dtype)


def _fox_decode(q, k_new, v_new, k_past, v_past, bias_t):
    b, width = q.shape
    p_len = k_past.shape[1]
    vec = pl.BlockSpec((1, 1, width), lambda i: (i, 0, 0))
    out = pl.pallas_call(
        _fox_decode_body,
        grid=(b,),
        in_specs=[vec, vec, vec,
                  pl.BlockSpec((1, p_len, width), lambda i: (i, 0, 0)),
                  pl.BlockSpec((1, p_len, width), lambda i: (i, 0, 0)),
                  pl.BlockSpec((1, FOX_HEADS, p_len), lambda i: (i, 0, 0))],
        out_specs=vec,
        out_shape=jax.ShapeDtypeStruct((b, 1, width), BF16),
        compiler_params=_cparams(("parallel",)),
        name="fox_decode",
    )(q.reshape(b, 1, width), k_new.reshape(b, 1, width), v_new.reshape(b, 1, width), k_past, v_past, bias_t)
    return out.reshape(b, width)


def kernel(x_prompt, x_sample, state_ret, cache_k, cache_v, cache_logf, page_table, c_prompt, c_sample,
           norm_mix, norm_ffn, ada_w, ada_b, ret_w_in, ret_w_out, fox_w_q, fox_w_o, norm_kv, ada_kv_w,
           ada_kv_b, kv_w, f_w, f_b, router_w, router_b, expert_w_gu, expert_b_gu, expert_w_down,
           expert_b_down, norm_final, ada_final_w, ada_final_b):
    d = D_MODEL
    bp, seq, _ = x_prompt.shape
    bs = x_sample.shape[0]
    tp = bp * seq
    past_len = page_table.shape[1] * cache_k.shape[1]

    ret_w_in_b = ret_w_in.astype(BF16)
    ret_w_out_b = ret_w_out.astype(BF16)
    fox_w_q_b = fox_w_q.astype(BF16)
    fox_w_o_b = fox_w_o.astype(BF16)
    kv_w_b = kv_w.astype(BF16)
    f_w_pad = jnp.zeros((d, LANES), BF16).at[:, :FOX_HEADS].set(f_w.astype(BF16))
    f_b_pad = jnp.zeros((LANES,), F32).at[:FOX_HEADS].set(f_b)
    router_w_pad = jnp.zeros((DEPTH, d, LANES), BF16).at[:, :, :N_EXPERTS].set(router_w.astype(BF16))
    router_b_pad = jnp.full((DEPTH, 1, LANES), -jnp.inf, F32).at[:, 0, :N_EXPERTS].set(router_b)
    w1 = _deinterleave_gu(expert_w_gu.reshape(DEPTH * N_EXPERTS, d, 2 * D_FF)).reshape(DEPTH, N_EXPERTS, d, 2 * D_FF)
    b1 = _regroup_gu_bias(expert_b_gu)[:, :, None, :]
    w2 = expert_w_down.astype(BF16)
    b2 = expert_b_down[:, :, None, :]

    ada_all_w = jnp.concatenate([ada_w[l] for l in range(DEPTH)] + [ada_kv_w, ada_final_w], axis=1).astype(BF16)
    ada_all_b = jnp.concatenate([ada_b[l] for l in range(DEPTH)] + [ada_kv_b, ada_final_b], axis=0)
    n_c = bs + bp
    n_c_pad = ((n_c + 7) // 8) * 8
    c_all = jnp.zeros((n_c_pad, d), F32).at[:bs].set(c_sample).at[bs:n_c].set(c_prompt)
    mods = _linear(c_all, ada_all_w, pre_silu=True, bias=ada_all_b, tm=n_c_pad, tn=2048, name="ada_params")

    def mod_s(col):
        return mods[:bs, col * d:(col + 1) * d]

    def mod_p(col):
        return mods[bs:n_c, col * d:(col + 1) * d].reshape(bp, 1, d)

    xp = x_prompt.reshape(tp, d)
    xs = x_sample.reshape(bs, d)
    sample_pos = past_len

    ret_states_p, ret_states_s = [], []
    kv_p = kv_s = None
    for layer in range(DEPTH):
        base = layer * N_MOD
        gain_mix = norm_mix[layer]
        mix_mod_p = (gain_mix, mod_p(base + 0), mod_p(base + 1))
        mix_mod_s = (gain_mix, mod_s(base + 0), mod_s(base + 1))
        if layer < N_A_LAYERS:
            proj_p = _linear(xp, ret_w_in_b[layer], mod=mix_mod_p, out_dtype=BF16, tn=2048,
                             rows_per_group=seq, name="ret_in_prompt")
            gated_p, st_p = _retention_prompt(proj_p, bp, seq)
            xp = _linear(gated_p, ret_w_out_b[layer], epi=(xp, mod_p(base + 2)), rows_per_group=seq,
                         name="ret_out_prompt")
            ret_states_p.append(st_p)
            proj_s = _linear(xs, ret_w_in_b[layer], mod=mix_mod_s, out_dtype=F32, tn=2048, name="ret_in_sample")
            gated_s, st_s = _retention_sample(proj_s, state_ret[layer], sample_pos)
            xs = _linear(gated_s, ret_w_out_b[layer], epi=(xs, mod_s(base + 2)), name="ret_out_sample")
            ret_states_s.append(st_s)
        else:
            j = layer - N_A_LAYERS
            k_p, v_p, bias_rows, k_pb, v_pb = kv_p
            q_p = _linear(xp, fox_w_q_b[j], mod=mix_mod_p, out_dtype=BF16, rows_per_group=seq, name="fox_q_prompt")
            att_p = _fox_prompt(q_p.reshape(bp, seq, d), k_pb, v_pb, bias_rows, bp, seq)
            xp = _linear(att_p.reshape(tp, d), fox_w_o_b[j], epi=(xp, mod_p(base + 2)), rows_per_group=seq,
                         name="fox_o_prompt")
            k_s, v_s, k_past, v_past, bias_t = kv_s
            q_s = _linear(xs, fox_w_q_b[j], mod=mix_mod_s, out_dtype=F32, name="fox_q_sample")
            att_s = _fox_decode(q_s, k_s, v_s, k_past, v_past, bias_t)
            xs = _linear(att_s, fox_w_o_b[j], epi=(xs, mod_s(base + 2)), name="fox_o_sample")

        gain_ffn = norm_ffn[layer]
        h_p, ti_p, tw_p = _router(xp, gain_ffn, mod_p(base + 3), mod_p(base + 4), router_w_pad[layer],
                                  router_b_pad[layer], tm=512, rows_per_group=seq)
        h_s, ti_s, tw_s = _router(xs, gain_ffn, mod_s(base + 3), mod_s(base + 4), router_w_pad[layer],
                                  router_b_pad[layer], tm=bs)
        topi = jnp.concatenate([ti_p[:, :TOP_K], ti_s[:, :TOP_K]], axis=0)
        topw = jnp.concatenate([tw_p[:, :TOP_K], tw_s[:, :TOP_K]], axis=0)
        g2_p = jnp.broadcast_to(mod_p(base + 5), (bp, seq, d)).reshape(tp, d)
        xp, xs = _moe_layer([xp, xs], [h_p, h_s], topi, topw, [g2_p, mod_s(base + 5)],
                            w1[layer], b1[layer], w2[layer], b2[layer])

        if layer == N_A_LAYERS - 1:
            kvb = DEPTH * N_MOD
            kv_mod_p = (norm_kv, mod_p(kvb + 0), mod_p(kvb + 1))
            kv_mod_s = (norm_kv, mod_s(kvb + 0), mod_s(kvb + 1))
            kvp = _linear(xp, kv_w_b, mod=kv_mod_p, rows_per_group=seq, name="kv_prompt")
            lf_p = _linear(xp, f_w_pad, mod=kv_mod_p, bias=f_b_pad, post="log_sigmoid", rows_per_group=seq,
                           name="logf_prompt")[:, :FOX_HEADS]
            k_p, v_p = kvp[:, :d], kvp[:, d:]
            logf_p = lf_p.reshape(bp, seq, FOX_HEADS)
            bias_rows = _fox_bias_rows(jnp.cumsum(logf_p, axis=1))
            kv_p = (k_p, v_p, bias_rows, k_p.astype(BF16).reshape(bp, seq, d), v_p.astype(BF16).reshape(bp, seq, d))

            kvs = _linear(xs, kv_w_b, mod=kv_mod_s, name="kv_sample")
            lf_s = _linear(xs, f_w_pad, mod=kv_mod_s, bias=f_b_pad, post="log_sigmoid",
                           name="logf_sample")[:, :FOX_HEADS]
            k_s, v_s = kvs[:, :d], kvs[:, d:]
            k_past, v_past = _compact_pages(page_table, cache_k, cache_v)
            lf_past = cache_logf[page_table].reshape(bs, past_len, FOX_HEADS).astype(F32)
            cs = jnp.cumsum(lf_past, axis=1)
            bias = (cs[:, -1:, :] - cs) + lf_s[:, None, :]
            kv_s = (k_s, v_s, k_past, v_past, jnp.transpose(bias, (0, 2, 1)))

    fb = DEPTH * N_MOD + 2
    y_p = _linear(xp, None, mod=(norm_final, mod_p(fb + 0), mod_p(fb + 1)), rows_per_group=seq, name="final_prompt")
    y_s = _linear(xs, None, mod=(norm_final, mod_s(fb + 0), mod_s(fb + 1)), name="final_sample")

    k_p, v_p = kv_p[0], kv_p[1]
    k_s, v_s = kv_s[0], kv_s[1]
    return (y_p.reshape(bp, seq, d),
            y_s.reshape(bs, 1, d),
            jnp.stack(ret_states_p),
            jnp.stack(ret_states_s),
            k_p.reshape(bp, seq, FOX_HEADS, FOX_HEAD_DIM),
            v_p.reshape(bp, seq, FOX_HEADS, FOX_HEAD_DIM),
            logf_p,
            k_s.reshape(bs, 1, FOX_HEADS, FOX_HEAD_DIM),
            v_s.reshape(bs, 1, FOX_HEADS, FOX_HEAD_DIM),
            lf_s.reshape(bs, 1, FOX_HEADS))
```

```python
import functools

import jax
import jax.numpy as jnp
import numpy as np
from jax import lax
from jax.experimental import pallas as pl
from jax.experimental.pallas import tpu as pltpu

F32 = jnp.float32
BF16 = jnp.bfloat16

D_MODEL = 1024
DEPTH = 4
N_A_LAYERS = 2
RET_HEADS = 4
RET_QK_DIM = 256
RET_V_DIM = 512
RET_QK_WIDTH = 1024
RET_V_WIDTH = 2048
RET_CHUNK = 128
ROPE_BASE = 10000.0
FOX_HEADS = 16
FOX_HEAD_DIM = 64
N_EXPERTS = 32
TOP_K = 4
D_FF = 1024
SWIGLU_LIMIT = 7.0
SWIGLU_ALPHA = 1.702
NORM_EPS = 1e-6
N_MOD = 6
MASK_VALUE = -1e30

V7X_VMEM_LIMIT = 56 * 1024 * 1024
LANES = 128


def _cparams(sem):
    return pltpu.CompilerParams(dimension_semantics=sem, vmem_limit_bytes=V7X_VMEM_LIMIT)


def _dot(a, b):
    return jnp.dot(a, b, preferred_element_type=F32)


def _dot_nt(a, b):
    return lax.dot_general(a, b, (((1,), (1,)), ((), ())), preferred_element_type=F32)


def _dot_tn(a, b):
    return lax.dot_general(a, b, (((0,), (0,)), ((), ())), preferred_element_type=F32)


def _modulated(x, gain, shift, scale):
    xf = x.astype(F32)
    y = xf * lax.rsqrt(jnp.mean(xf * xf, axis=-1, keepdims=True) + NORM_EPS)
    return (y * gain) * (1.0 + scale) + shift


def _log_sigmoid(z):
    return jnp.minimum(z, 0.0) - jnp.log1p(jnp.exp(-jnp.abs(z)))


def _linear_body(*refs, mod, pre_silu, has_bias, epi, post, n_j, only_mod):
    it = iter(refs)
    x_ref = next(it)
    if mod:
        gain_ref, shift_ref, scale_ref = next(it), next(it), next(it)
    if not only_mod:
        w_ref = next(it)
    b_ref = next(it) if has_bias else None
    if epi:
        res_ref, gate_ref = next(it), next(it)
    o_ref = next(it)
    h_ref = next(it) if n_j > 1 else None

    def prologue():
        x = x_ref[...]
        if mod:
            return _modulated(x, gain_ref[...], shift_ref[0], scale_ref[0])
        if pre_silu:
            xf = x.astype(F32)
            return xf * jax.nn.sigmoid(xf)
        return x

    if only_mod:
        o_ref[...] = prologue().astype(o_ref.dtype)
        return

    if n_j > 1:
        @pl.when(pl.program_id(1) == 0)
        def _():
            h_ref[...] = prologue().astype(BF16)
        h = h_ref[...]
    else:
        h = prologue().astype(BF16)

    acc = _dot(h, w_ref[...])
    if has_bias:
        acc = acc + b_ref[...]
    if post == "log_sigmoid":
        acc = _log_sigmoid(acc)
    if epi:
        acc = res_ref[...] + gate_ref[0] * acc
    o_ref[...] = acc.astype(o_ref.dtype)


def _linear(x, w=None, *, mod=None, pre_silu=False, bias=None, epi=None, post=None,
            out_dtype=F32, tm=512, tn=1024, rows_per_group=None, name="linear"):
    m, k = x.shape
    only_mod = w is None
    n = k if only_mod else w.shape[1]
    tm = min(tm, m)
    tn = n if only_mod else min(tn, n)
    assert m % tm == 0 and n % tn == 0
    n_i, n_j = m // tm, n // tn
    if only_mod:
        n_j = 1

    def row_or_group(arr, width, col_tiled):
        if arr.ndim == 3:
            tiles_per_group = rows_per_group // tm
            assert rows_per_group % tm == 0
            if col_tiled:
                return arr, pl.BlockSpec((1, 1, tn), lambda i, j: (i // tiles_per_group, 0, j))
            return arr, pl.BlockSpec((1, 1, width), lambda i, j: (i // tiles_per_group, 0, 0))
        arr3 = arr.reshape(n_i, tm, width)
        if col_tiled:
            return arr3, pl.BlockSpec((1, tm, tn), lambda i, j: (i, 0, j))
        return arr3, pl.BlockSpec((1, tm, width), lambda i, j: (i, 0, 0))

    args = [x]
    specs = [pl.BlockSpec((tm, k), lambda i, j: (i, 0))]
    if mod is not None:
        gain, shift, scale = mod
        args.append(gain.reshape(1, k).astype(F32))
        specs.append(pl.BlockSpec((1, k), lambda i, j: (0, 0)))
        for a in (shift, scale):
            a3, sp = row_or_group(a, k, False)
            args.append(a3)
            specs.append(sp)
    if not only_mod:
        args.append(w)
        specs.append(pl.BlockSpec((k, tn), lambda i, j: (0, j)))
    if bias is not None:
        args.append(bias.reshape(1, n).astype(F32))
        specs.append(pl.BlockSpec((1, tn), lambda i, j: (0, j)))
    if epi is not None:
        res, gate = epi
        args.append(res)
        specs.append(pl.BlockSpec((tm, tn), lambda i, j: (i, j)))
        g3, sp = row_or_group(gate, n, True)
        args.append(g3)
        specs.append(sp)

    body = functools.partial(_linear_body, mod=mod is not None, pre_silu=pre_silu,
                             has_bias=bias is not None, epi=epi is not None, post=post,
                             n_j=n_j, only_mod=only_mod)
    scratch = [pltpu.VMEM((tm, k), BF16)] if n_j > 1 else []
    return pl.pallas_call(
        body,
        grid=(n_i, n_j),
        in_specs=specs,
        out_specs=pl.BlockSpec((tm, tn), lambda i, j: (i, j)),
        out_shape=jax.ShapeDtypeStruct((m, n), out_dtype),
        scratch_shapes=scratch,
        compiler_params=_cparams(("parallel", "arbitrary")),
        name=name,
    )(*args)


def _router_body(x_ref, gain_ref, shift_ref, scale_ref, wr_ref, br_ref, h_ref, ti_ref, tw_ref):
    h = _modulated(x_ref[...], gain_ref[...], shift_ref[0], scale_ref[0]).astype(BF16)
    h_ref[...] = h
    logits = _dot(h, wr_ref[...]) + br_ref[...]
    lane = lax.broadcasted_iota(jnp.int32, logits.shape, 1).astype(F32)
    vals, idxs = [], []
    cur = logits
    for _ in range(TOP_K):
        mx = jnp.max(cur, axis=-1, keepdims=True)
        idx = jnp.min(jnp.where(cur == mx, lane, float(LANES)), axis=-1, keepdims=True)
        vals.append(mx)
        idxs.append(idx)
        cur = jnp.where(lane == idx, -jnp.inf, cur)
    exps = [jnp.exp(v - vals[0]) for v in vals]
    denom = exps[0] + exps[1] + exps[2] + exps[3]
    ti = jnp.zeros(logits.shape, F32)
    tw = jnp.zeros(logits.shape, F32)
    for kk in range(TOP_K):
        ti = jnp.where(lane == float(kk), idxs[kk], ti)
        tw = jnp.where(lane == float(kk), exps[kk] / denom, tw)
    ti_ref[...] = ti.astype(jnp.int32)
    tw_ref[...] = tw


def _router(x, gain, shift, scale, wr_pad, br_pad, *, tm, total_rows, row_offset=0, rows_per_group=None,
            bufs=None):
    m, k = x.shape
    tm = min(tm, m)
    n_i = m // tm
    assert row_offset % tm == 0
    off = row_offset // tm
    if shift.ndim == 3:
        tpg = rows_per_group // tm
        mod_spec = pl.BlockSpec((1, 1, k), lambda i: (i // tpg, 0, 0))
        sh3, sc3 = shift, scale
    else:
        mod_spec = pl.BlockSpec((1, tm, k), lambda i: (i, 0, 0))
        sh3, sc3 = shift.reshape(n_i, tm, k), scale.reshape(n_i, tm, k)
    args = [x, gain.reshape(1, k).astype(F32), sh3, sc3, wr_pad, br_pad]
    in_specs = [pl.BlockSpec((tm, k), lambda i: (i, 0)),
                pl.BlockSpec((1, k), lambda i: (0, 0)),
                mod_spec, mod_spec,
                pl.BlockSpec((k, LANES), lambda i: (0, 0)),
                pl.BlockSpec((1, LANES), lambda i: (0, 0))]
    aliases = {}
    body = _router_body
    if bufs is not None:
        aliases = {len(args) + j: j for j in range(3)}
        args += list(bufs)
        in_specs += [pl.BlockSpec(memory_space=pl.ANY)] * 3

        def body(*refs):
            _router_body(*refs[:6], *refs[9:])

    return pl.pallas_call(
        body,
        grid=(n_i,),
        in_specs=in_specs,
        out_specs=[pl.BlockSpec((tm, k), lambda i: (i + off, 0)),
                   pl.BlockSpec((tm, LANES), lambda i: (i + off, 0)),
                   pl.BlockSpec((tm, LANES), lambda i: (i + off, 0))],
        out_shape=[jax.ShapeDtypeStruct((total_rows, k), BF16),
                   jax.ShapeDtypeStruct((total_rows, LANES), jnp.int32),
                   jax.ShapeDtypeStruct((total_rows, LANES), F32)],
        input_output_aliases=aliases,
        compiler_params=_cparams(("parallel",)),
        name="moe_router",
    )(*args)


MOE_TILE = 512
GU_GROUP = 2 * LANES


def _deint_body(w_ref, p_ref, o_ref):
    for c in range(w_ref.shape[2] // GU_GROUP):
        cols = slice(c * GU_GROUP, (c + 1) * GU_GROUP)
        o_ref[0, :, cols] = _dot(w_ref[0, :, cols].astype(BF16), p_ref[...]).astype(o_ref.dtype)


def _deinterleave_gu(w, rows=1024):
    e, k, n = w.shape
    src = jnp.arange(GU_GROUP)
    dst = (src % 2) * LANES + src // 2
    perm = jnp.zeros((GU_GROUP, GU_GROUP), BF16).at[src, dst].set(1.0)
    return pl.pallas_call(
        _deint_body,
        grid=(e, k // rows),
        in_specs=[pl.BlockSpec((1, rows, n), lambda i, j: (i, j, 0)),
                  pl.BlockSpec((GU_GROUP, GU_GROUP), lambda i, j: (0, 0))],
        out_specs=pl.BlockSpec((1, rows, n), lambda i, j: (i, j, 0)),
        out_shape=jax.ShapeDtypeStruct((e, k, n), BF16),
        compiler_params=_cparams(("parallel", "parallel")),
        name="moe_weight_regroup",
    )(w, perm)


def _regroup_gu_bias(b):
    lead = b.shape[:-1]
    g = b.reshape(*lead, -1, LANES, 2)
    return jnp.swapaxes(g, -1, -2).reshape(*lead, -1)


def _gmm_body(te_ref, tv_ref, hs_ref, w1_ref, b1_ref, w2_ref, b2_ref, o_ref):
    i = pl.program_id(0)

    @pl.when(tv_ref[i] > 0)
    def _():
        gu = _dot(hs_ref[...], w1_ref[0]) + b1_ref[0]
        acts = []
        for c in range(2 * D_FF // GU_GROUP):
            glu = jnp.minimum(gu[:, c * GU_GROUP:c * GU_GROUP + LANES], SWIGLU_LIMIT)
            lin = jnp.clip(gu[:, c * GU_GROUP + LANES:(c + 1) * GU_GROUP], -SWIGLU_LIMIT, SWIGLU_LIMIT)
            acts.append((glu * jax.nn.sigmoid(SWIGLU_ALPHA * glu) * (lin + 1.0)).astype(BF16))
        act = jnp.concatenate(acts, axis=1)
        y = _dot(act, w2_ref[0]) + b2_ref[0]
        o_ref[...] = y.astype(o_ref.dtype)

    @pl.when(tv_ref[i] == 0)
    def _():
        o_ref[...] = jnp.zeros_like(o_ref)


def _moe_gmm(tile_expert, tile_valid, tile_block, hs, w1, b1, w2, b2):
    p_rows, d = hs.shape
    n_tiles = p_rows // MOE_TILE
    grid_spec = pltpu.PrefetchScalarGridSpec(
        num_scalar_prefetch=3,
        grid=(n_tiles,),
        in_specs=[
            pl.BlockSpec((MOE_TILE, d), lambda i, te, tv, tb: (tb[i], 0)),
            pl.BlockSpec((1, d, 2 * D_FF), lambda i, te, tv, tb: (te[i], 0, 0)),
            pl.BlockSpec((1, 1, 2 * D_FF), lambda i, te, tv, tb: (te[i], 0, 0)),
            pl.BlockSpec((1, D_FF, d), lambda i, te, tv, tb: (te[i], 0, 0)),
            pl.BlockSpec((1, 1, d), lambda i, te, tv, tb: (te[i], 0, 0)),
        ],
        out_specs=pl.BlockSpec((MOE_TILE, d), lambda i, te, tv, tb: (i, 0)),
    )

    def body(te_ref, tv_ref, tb_ref, *rest):
        _gmm_body(te_ref, tv_ref, *rest)

    return pl.pallas_call(
        body,
        grid_spec=grid_spec,
        out_shape=jax.ShapeDtypeStruct((p_rows, d), BF16),
        compiler_params=_cparams(("arbitrary",)),
        name="moe_gmm",
    )(tile_expert, tile_valid, tile_block, hs, w1, b1, w2, b2)


def _combine_body(x_ref, y_ref, tw_ref, g_ref, o_ref):
    w = tw_ref[...]
    acc = y_ref[0].astype(F32) * w[:, 0:1]
    for kk in range(1, TOP_K):
        acc = acc + y_ref[kk].astype(F32) * w[:, kk:kk + 1]
    o_ref[...] = x_ref[...] + g_ref[0] * acc


def _moe_combine(x, yk, tw_all, gate, *, tm, row_offset, rows_per_group=None):
    m, d = x.shape
    tm = min(tm, m)
    n_i = m // tm
    off = row_offset // tm
    if gate.ndim == 3:
        tpg = rows_per_group // tm
        g3, g_spec = gate, pl.BlockSpec((1, 1, d), lambda i: (i // tpg, 0, 0))
    else:
        g3, g_spec = gate.reshape(n_i, tm, d), pl.BlockSpec((1, tm, d), lambda i: (i, 0, 0))
    return pl.pallas_call(
        _combine_body,
        grid=(n_i,),
        in_specs=[pl.BlockSpec((tm, d), lambda i: (i, 0)),
                  pl.BlockSpec((TOP_K, tm, d), lambda i: (0, i + off, 0)),
                  pl.BlockSpec((tm, LANES), lambda i: (i + off, 0)),
                  g_spec],
        out_specs=pl.BlockSpec((tm, d), lambda i: (i, 0)),
        out_shape=jax.ShapeDtypeStruct((m, d), F32),
        compiler_params=_cparams(("parallel",)),
        name="moe_combine",
    )(x, yk, tw_all, g3)


def _moe_sorted_rows(t):
    return ((t * TOP_K + N_EXPERTS * (MOE_TILE - 1)) // MOE_TILE + 1) * MOE_TILE


def _moe_layer(x_parts, h_all, ti_all, tw_all, gates2, rows_per_group, w1, b1, w2, b2):
    t = sum(xp.shape[0] for xp in x_parts)
    n_slots = t * TOP_K
    flat_e = ti_all[:t, :TOP_K].T.reshape(n_slots)
    onehot = (flat_e[:, None] == jnp.arange(N_EXPERTS, dtype=jnp.int32)[None, :]).astype(jnp.int32)
    csum = jnp.cumsum(onehot, axis=0)
    counts = csum[-1]
    rank = jnp.sum(csum * onehot, axis=1) - 1
    padded = ((counts + MOE_TILE - 1) // MOE_TILE) * MOE_TILE
    poffs = jnp.cumsum(padded) - padded
    total_padded = jnp.sum(padded)
    p_rows = _moe_sorted_rows(t)
    n_tiles = p_rows // MOE_TILE
    tile_start = jnp.arange(n_tiles, dtype=jnp.int32) * MOE_TILE
    tile_valid = (tile_start < total_padded).astype(jnp.int32)
    pends = poffs + padded
    tile_expert = jnp.minimum(jnp.sum((pends[None, :] <= tile_start[:, None]).astype(jnp.int32), axis=1),
                              N_EXPERTS - 1)
    last_valid = jnp.maximum(total_padded // MOE_TILE - 1, 0).astype(jnp.int32)
    tile_block = jnp.minimum(jnp.arange(n_tiles, dtype=jnp.int32), last_valid)
    tile_expert = jnp.where(tile_valid > 0, tile_expert, tile_expert[last_valid])
    inv = poffs[flat_e] + rank
    row_token = jnp.zeros((p_rows,), jnp.int32).at[inv].set(jnp.arange(n_slots, dtype=jnp.int32) % t)

    hs = jnp.take(h_all, row_token, axis=0)
    ys = _moe_gmm(tile_expert, tile_valid, tile_block, hs, w1, b1, w2, b2)
    yk = jnp.take(ys, inv, axis=0).reshape(TOP_K, t, -1)
    outs, start = [], 0
    for xp, g2, rpg in zip(x_parts, gates2, rows_per_group):
        r = xp.shape[0]
        outs.append(_moe_combine(xp, yk, tw_all, g2, tm=512, row_offset=start, rows_per_group=rpg))
        start += r
    return outs


def _rope_tables(pos):
    half = RET_QK_DIM // 2
    inv = 1.0 / (ROPE_BASE ** jnp.linspace(0.0, 1.0, half, dtype=F32))
    ang = jnp.repeat(pos.astype(F32)[:, None] * inv[None, :], 2, axis=-1)
    cos, sin = jnp.cos(ang), jnp.sin(ang)
    even = (jnp.arange(RET_QK_DIM) % 2 == 0)[None, :]
    return cos, jnp.where(even, -sin, 0.0), jnp.where(even, 0.0, sin)


def _rope(x, cos, sin_next, sin_prev):
    n = x.shape[-1]
    return x * cos + pltpu.roll(x, n - 1, 1) * sin_next + pltpu.roll(x, 1, 1) * sin_prev


def _log_gamma():
    return jnp.log1p(-jnp.exp2(-5.0 - jnp.arange(RET_HEADS, dtype=F32)))


def _ret_prompt_body(q_ref, k_ref, v_ref, g_ref, cos_ref, sn_ref, sp_ref, dec_ref, rd_ref,
                     o_ref, st_ref, state):
    c = pl.program_id(2)

    @pl.when(c == 0)
    def _():
        state[...] = jnp.zeros_like(state)

    cos, sn, sp = cos_ref[...], sn_ref[...], sp_ref[...]
    q = _rope(q_ref[0].astype(F32), cos, sn, sp)
    k = _rope(k_ref[0].astype(F32), cos, sn, sp) * (RET_QK_DIM ** -0.5)
    v = v_ref[0]
    rd = rd_ref[0]
    q_dec = (q * rd[:, 0:1]).astype(BF16)
    k_dec = (k * rd[:, 1:2]).astype(BF16)
    inner = (_dot_nt(q.astype(BF16), k.astype(BF16)) * dec_ref[0]).astype(BF16)
    s_old = state[...]
    out = _dot(inner, v) + _dot(q_dec, s_old.astype(BF16))
    state[...] = rd[0:1, 2:3] * s_old + _dot_tn(k_dec, v)
    out = out * lax.rsqrt(jnp.mean(out * out, axis=-1, keepdims=True) + NORM_EPS)
    g = g_ref[0].astype(F32)
    o_ref[0] = (g * jax.nn.sigmoid(g) * out).astype(o_ref.dtype)

    @pl.when(c == pl.num_programs(2) - 1)
    def _():
        st_ref[0, 0] = state[...]


def _retention_prompt(proj, batch, seq):
    c = RET_CHUNK
    n_chunks = seq // c
    proj3 = proj.reshape(batch, seq, -1)
    cos, sn, sp = _rope_tables(jnp.arange(seq, dtype=jnp.int32))
    lg = _log_gamma()
    idx = jnp.arange(c, dtype=F32)
    diff = idx[:, None] - idx[None, :]
    decay = jnp.where(diff >= 0, jnp.exp(jnp.maximum(diff, 0.0)[None] * lg[:, None, None]), 0.0)
    qd = jnp.exp((idx + 1.0)[None, :] * lg[:, None])
    kd = jnp.exp((c - 1.0 - idx)[None, :] * lg[:, None])
    cd = jnp.broadcast_to(jnp.exp(c * lg)[:, None], (RET_HEADS, c))
    rd = jnp.zeros((RET_HEADS, c, LANES), F32).at[:, :, 0].set(qd).at[:, :, 1].set(kd).at[:, :, 2].set(cd)
    nqk = RET_QK_WIDTH // RET_QK_DIM
    out, st = pl.pallas_call(
        _ret_prompt_body,
        grid=(batch, RET_HEADS, n_chunks),
        in_specs=[
            pl.BlockSpec((1, c, RET_QK_DIM), lambda b, h, i: (b, i, h)),
            pl.BlockSpec((1, c, RET_QK_DIM), lambda b, h, i: (b, i, nqk + h)),
            pl.BlockSpec((1, c, RET_V_DIM), lambda b, h, i: (b, i, nqk + h)),
            pl.BlockSpec((1, c, RET_V_DIM), lambda b, h, i: (b, i, 2 * nqk + h)),
            pl.BlockSpec((c, RET_QK_DIM), lambda b, h, i: (i, 0)),
            pl.BlockSpec((c, RET_QK_DIM), lambda b, h, i: (i, 0)),
            pl.BlockSpec((c, RET_QK_DIM), lambda b, h, i: (i, 0)),
            pl.BlockSpec((1, c, c), lambda b, h, i: (h, 0, 0)),
            pl.BlockSpec((1, c, LANES), lambda b, h, i: (h, 0, 0)),
        ],
        out_specs=[
            pl.BlockSpec((1, c, RET_V_DIM), lambda b, h, i: (b, i, h)),
            pl.BlockSpec((1, 1, RET_QK_DIM, RET_V_DIM), lambda b, h, i: (b, h, 0, 0)),
        ],
        out_shape=[jax.ShapeDtypeStruct((batch, seq, RET_V_WIDTH), BF16),
                   jax.ShapeDtypeStruct((batch, RET_HEADS, RET_QK_DIM, RET_V_DIM), F32)],
        scratch_shapes=[pltpu.VMEM((RET_QK_DIM, RET_V_DIM), F32)],
        compiler_params=_cparams(("parallel", "parallel", "arbitrary")),
        name="retention_prompt",
    )(proj3, proj3, proj3, proj3, cos, sn, sp, decay, rd)
    return out.reshape(batch * seq, RET_V_WIDTH), st


def _ret_sample_body(p_ref, st_ref, rope_ref, gam_ref, o_ref, ns_ref):
    cos, sn, sp = rope_ref[0:1, :], rope_ref[1:2, :], rope_ref[2:3, :]
    row0 = lax.broadcasted_iota(jnp.int32, (8, RET_QK_DIM), 0) == 0
    for h in range(RET_HEADS):
        gamma = gam_ref[h:h + 1, 0:1]
        q = p_ref[0, :, h * RET_QK_DIM:(h + 1) * RET_QK_DIM].astype(F32)
        k = p_ref[0, :, RET_QK_WIDTH + h * RET_QK_DIM:RET_QK_WIDTH + (h + 1) * RET_QK_DIM].astype(F32)
        v0 = 2 * RET_QK_WIDTH + h * RET_V_DIM
        v = p_ref[0, :, v0:v0 + RET_V_DIM].astype(F32)
        g0 = 2 * RET_QK_WIDTH + RET_V_WIDTH + h * RET_V_DIM
        g = p_ref[0, :, g0:g0 + RET_V_DIM].astype(F32)
        q = _rope(jnp.broadcast_to(q, (8, RET_QK_DIM)), cos, sn, sp)
        k = _rope(jnp.broadcast_to(k, (8, RET_QK_DIM)), cos, sn, sp) * (RET_QK_DIM ** -0.5)
        qb = q.astype(BF16)
        kb = k.astype(BF16)
        vb = v.astype(BF16)
        s_old = st_ref[0, h]
        qk = jnp.sum(qb.astype(F32) * kb.astype(F32), axis=-1, keepdims=True)
        out = (qk.astype(BF16).astype(F32) * vb.astype(F32)
               + _dot((q * gamma).astype(BF16), s_old.astype(BF16)))
        k_row0 = jnp.where(row0, kb, jnp.zeros_like(kb))
        v8 = jnp.broadcast_to(vb, (8, RET_V_DIM))
        ns_ref[0, h] = gamma * s_old + _dot_tn(k_row0, v8)
        out = out[0:1]
        out = out * lax.rsqrt(jnp.mean(out * out, axis=-1, keepdims=True) + NORM_EPS)
        o_ref[0, :, h * RET_V_DIM:(h + 1) * RET_V_DIM] = (g * jax.nn.sigmoid(g) * out).astype(o_ref.dtype)


def _retention_sample(proj, state, pos):
    b = proj.shape[0]
    cos, sn, sp = _rope_tables(jnp.full((1,), pos, jnp.int32))
    rope = jnp.zeros((8, RET_QK_DIM), F32).at[0].set(cos[0]).at[1].set(sn[0]).at[2].set(sp[0])
    gam = jnp.broadcast_to(jnp.exp(_log_gamma())[:, None], (RET_HEADS, LANES))
    width = proj.shape[1]
    out, ns = pl.pallas_call(
        _ret_sample_body,
        grid=(b,),
        in_specs=[
            pl.BlockSpec((1, 1, width), lambda i: (i, 0, 0)),
            pl.BlockSpec((1, RET_HEADS, RET_QK_DIM, RET_V_DIM), lambda i: (i, 0, 0, 0)),
            pl.BlockSpec((8, RET_QK_DIM), lambda i: (0, 0)),
            pl.BlockSpec((RET_HEADS, LANES), lambda i: (0, 0)),
        ],
        out_specs=[
            pl.BlockSpec((1, 1, RET_V_WIDTH), lambda i: (i, 0, 0)),
            pl.BlockSpec((1, RET_HEADS, RET_QK_DIM, RET_V_DIM), lambda i: (i, 0, 0, 0)),
        ],
        out_shape=[jax.ShapeDtypeStruct((b, 1, RET_V_WIDTH), BF16),
                   jax.ShapeDtypeStruct(state.shape, state.dtype)],
        compiler_params=_cparams(("parallel",)),
        name="retention_sample",
    )(proj.reshape(b, 1, width), state, rope, gam)
    return out.reshape(b, RET_V_WIDTH), ns


FOX_TQ = 1024


def _fox_prompt_body(qi_ref, ki_ref, q_ref, k_ref, v_ref, fk_ref, o_ref, qa_s, m_s, acc_s):
    p = pl.program_id(2)
    qi, ki = qi_ref[p], ki_ref[p]
    tq = FOX_TQ
    lane = lax.broadcasted_iota(jnp.int32, (tq, LANES), 1)

    def head_lanes(hh):
        return (lane >= hh * FOX_HEAD_DIM) & (lane < (hh + 1) * FOX_HEAD_DIM)

    @pl.when(ki == 0)
    def _():
        q2 = q_ref[0].astype(F32) * (FOX_HEAD_DIM ** -0.5)
        for hh in range(2):
            qa_s[hh] = jnp.where(head_lanes(hh), q2, 0.0).astype(BF16)
        m_s[...] = jnp.full_like(m_s, MASK_VALUE)
        acc_s[...] = jnp.zeros_like(acc_s)

    def step(masked):
        k2 = k_ref[0]
        v2 = v_ref[0].astype(F32)
        if masked:
            row = lax.broadcasted_iota(jnp.int32, (tq, tq), 0)
            col = lax.broadcasted_iota(jnp.int32, (tq, tq), 1)
            keep = col <= row
        for hh in range(2):
            v_aug = jnp.where(head_lanes(hh), v2, 1.0).astype(BF16)
            s = _dot_nt(qa_s[hh], k2) - fk_ref[0, hh]
            if masked:
                s = jnp.where(keep, s, MASK_VALUE)
            m_prev = m_s[hh]
            m_next = jnp.maximum(m_prev, jnp.max(s, axis=1, keepdims=True))
            alpha = jnp.exp(m_prev - m_next)
            pr = jnp.exp(s - jnp.tile(m_next, (1, tq // LANES)))
            acc_s[hh] = alpha * acc_s[hh] + _dot(pr.astype(BF16), v_aug)
            m_s[hh] = m_next

    @pl.when(ki < qi)
    def _():
        step(False)

    @pl.when(ki == qi)
    def _():
        step(True)
        a0, a1 = acc_s[0], acc_s[1]
        o0 = a0 / pltpu.roll(a0, FOX_HEAD_DIM, 1)
        o1 = a1 / pltpu.roll(a1, FOX_HEAD_DIM, 1)
        o_ref[0] = jnp.where(lane < FOX_HEAD_DIM, o0, o1).astype(o_ref.dtype)


def _fox_prompt(q, k, v, fcum_t, batch, seq):
    tq = FOX_TQ
    n_q = seq // tq
    pairs = [(i, j) for i in range(n_q) for j in range(i + 1)]
    qi_tab = jnp.asarray([p[0] for p in pairs], jnp.int32)
    ki_tab = jnp.asarray([p[1] for p in pairs], jnp.int32)
    n_hp = FOX_HEADS // 2
    grid_spec = pltpu.PrefetchScalarGridSpec(
        num_scalar_prefetch=2,
        grid=(batch, n_hp, len(pairs)),
        in_specs=[
            pl.BlockSpec((1, tq, LANES), lambda b, h, p, qi, ki: (b, qi[p], h)),
            pl.BlockSpec((1, tq, LANES), lambda b, h, p, qi, ki: (b, ki[p], h)),
            pl.BlockSpec((1, tq, LANES), lambda b, h, p, qi, ki: (b, ki[p], h)),
            pl.BlockSpec((1, 2, 1, tq), lambda b, h, p, qi, ki: (b, h, 0, ki[p])),
        ],
        out_specs=pl.BlockSpec((1, tq, LANES), lambda b, h, p, qi, ki: (b, qi[p], h)),
        scratch_shapes=[pltpu.VMEM((2, tq, LANES), BF16), pltpu.VMEM((2, tq, LANES), F32),
                        pltpu.VMEM((2, tq, LANES), F32)],
    )
    return pl.pallas_call(
        _fox_prompt_body,
        grid_spec=grid_spec,
        out_shape=jax.ShapeDtypeStruct((batch, seq, FOX_HEADS * FOX_HEAD_DIM), BF16),
        compiler_params=_cparams(("parallel", "parallel", "arbitrary")),
        name="fox_prompt",
    )(qi_tab, ki_tab, q, k, v, fcum_t)


DEC_PAGES = 8


def _fox_decode_body(pt_ref, q_ref, kn_ref, vn_ref, bias_ref, *rest):
    k_refs, v_refs = rest[:DEC_PAGES], rest[DEC_PAGES:2 * DEC_PAGES]
    o_ref, m_s, l_s, acc_s = rest[2 * DEC_PAGES:]
    g = pl.program_id(1)
    width = FOX_HEADS * FOX_HEAD_DIM
    page = k_refs[0].shape[-1]
    lane = lax.broadcasted_iota(jnp.int32, (FOX_HEADS, width), 1)
    row = lax.broadcasted_iota(jnp.int32, (FOX_HEADS, width), 0)
    own = (lane >= row * FOX_HEAD_DIM) & (lane < (row + 1) * FOX_HEAD_DIM)
    scale = FOX_HEAD_DIM ** -0.5
    q_row = jnp.broadcast_to(q_ref[0].astype(F32), (FOX_HEADS, width))
    q_blk = jnp.where(own, q_row, 0.0).astype(BF16)

    @pl.when(g == 0)
    def _():
        m_s[...] = jnp.full_like(m_s, MASK_VALUE)
        l_s[...] = jnp.zeros_like(l_s)
        acc_s[...] = jnp.zeros_like(acc_s)

    k_t = jnp.concatenate([r[0].reshape(width, page).astype(BF16) for r in k_refs], axis=1)
    s = _dot(q_blk, k_t) * scale + bias_ref[0]
    m_prev = m_s[...]
    m_next = jnp.maximum(m_prev, jnp.max(s, axis=-1, keepdims=True))
    alpha = jnp.exp(m_prev - m_next)
    pr = jnp.exp(s - m_next)
    l_s[...] = alpha * l_s[...] + jnp.sum(pr, axis=-1, keepdims=True)
    prb = pr.astype(BF16)
    pv = _dot_nt(prb[:, 0:page], v_refs[0][0].reshape(width, page).astype(BF16))
    for i in range(1, DEC_PAGES):
        pv = pv + _dot_nt(prb[:, i * page:(i + 1) * page], v_refs[i][0].reshape(width, page).astype(BF16))
    acc_s[...] = alpha * acc_s[...] + pv
    m_s[...] = m_next

    @pl.when(g == pl.num_programs(1) - 1)
    def _():
        knb = kn_ref[0].astype(BF16).astype(F32)
        s_new = jnp.sum(q_blk.astype(F32) * knb, axis=-1, keepdims=True) * scale
        m_fin = jnp.maximum(m_next, s_new)
        a = jnp.exp(m_next - m_fin)
        p_new = jnp.exp(s_new - m_fin)
        denom = a * l_s[...] + p_new
        vnb = vn_ref[0].astype(BF16).astype(F32)
        o_full = (a * acc_s[...] + p_new.astype(BF16).astype(F32) * vnb) / denom
        o_ref[0] = jnp.sum(jnp.where(own, o_full, 0.0), axis=0, keepdims=True).astype(o_ref.dtype)


def _fox_decode(q, k_new, v_new, cache_kt, cache_vt, page_table, bias_t):
    b, width = q.shape
    n_pages = page_table.shape[1]
    _, heads, hd, page = cache_kt.shape
    n_groups = n_pages // DEC_PAGES
    vec = pl.BlockSpec((1, 1, width), lambda i, g, pt: (i, 0, 0))

    def page_spec(j):
        return pl.BlockSpec((1, heads, hd, page), lambda i, g, pt: (pt[i, g * DEC_PAGES + j], 0, 0, 0))

    grid_spec = pltpu.PrefetchScalarGridSpec(
        num_scalar_prefetch=1,
        grid=(b, n_groups),
        in_specs=[vec, vec, vec,
                  pl.BlockSpec((1, FOX_HEADS, DEC_PAGES * page), lambda i, g, pt: (i, 0, g))]
                 + [page_spec(j) for j in range(DEC_PAGES)] * 2,
        out_specs=vec,
        scratch_shapes=[pltpu.VMEM((FOX_HEADS, 1), F32), pltpu.VMEM((FOX_HEADS, 1), F32),
                        pltpu.VMEM((FOX_HEADS, width), F32)],
    )
    out = pl.pallas_call(
        _fox_decode_body,
        grid_spec=grid_spec,
        out_shape=jax.ShapeDtypeStruct((b, 1, width), BF16),
        compiler_params=_cparams(("parallel", "arbitrary")),
        name="fox_decode",
    )(page_table, q.reshape(b, 1, width), k_new.reshape(b, 1, width), v_new.reshape(b, 1, width), bias_t,
      *([cache_kt] * DEC_PAGES), *([cache_vt] * DEC_PAGES))
    return out.reshape(b, width)


def kernel(x_prompt, x_sample, state_ret, cache_k, cache_v, cache_logf, page_table, c_prompt, c_sample,
           norm_mix, norm_ffn, ada_w, ada_b, ret_w_in, ret_w_out, fox_w_q, fox_w_o, norm_kv, ada_kv_w,
           ada_kv_b, kv_w, f_w, f_b, router_w, router_b, expert_w_gu, expert_b_gu, expert_w_down,
           expert_b_down, norm_final, ada_final_w, ada_final_b):
    d = D_MODEL
    bp, seq, _ = x_prompt.shape
    bs = x_sample.shape[0]
    tp = bp * seq
    past_len = page_table.shape[1] * cache_k.shape[1]

    ret_w_in_b = ret_w_in.astype(BF16)
    ret_w_out_b = ret_w_out.astype(BF16)
    fox_w_q_b = fox_w_q.astype(BF16)
    fox_w_o_b = fox_w_o.astype(BF16)
    kv_w_b = kv_w.astype(BF16)
    f_w_pad = jnp.zeros((d, LANES), BF16).at[:, :FOX_HEADS].set(f_w.astype(BF16))
    f_b_pad = jnp.zeros((LANES,), F32).at[:FOX_HEADS].set(f_b)
    router_w_pad = jnp.zeros((DEPTH, d, LANES), BF16).at[:, :, :N_EXPERTS].set(router_w.astype(BF16))
    router_b_pad = jnp.full((DEPTH, 1, LANES), -jnp.inf, F32).at[:, 0, :N_EXPERTS].set(router_b)
    w1 = _deinterleave_gu(expert_w_gu.reshape(DEPTH * N_EXPERTS, d, 2 * D_FF)).reshape(DEPTH, N_EXPERTS, d, 2 * D_FF)
    b1 = _regroup_gu_bias(expert_b_gu)[:, :, None, :]
    w2 = expert_w_down.astype(BF16)
    b2 = expert_b_down[:, :, None, :]

    ada_all_w = jnp.concatenate([ada_w[l] for l in range(DEPTH)] + [ada_kv_w, ada_final_w], axis=1).astype(BF16)
    ada_all_b = jnp.concatenate([ada_b[l] for l in range(DEPTH)] + [ada_kv_b, ada_final_b], axis=0)
    n_c = bs + bp
    n_c_pad = ((n_c + 7) // 8) * 8
    c_all = jnp.zeros((n_c_pad, d), F32).at[:bs].set(c_sample).at[bs:n_c].set(c_prompt)
    mods = _linear(c_all, ada_all_w, pre_silu=True, bias=ada_all_b, tm=n_c_pad, tn=2048, name="ada_params")

    def mod_s(col):
        return mods[:bs, col * d:(col + 1) * d]

    def mod_p(col):
        return mods[bs:n_c, col * d:(col + 1) * d].reshape(bp, 1, d)

    xp = x_prompt.reshape(tp, d)
    xs = x_sample.reshape(bs, d)
    sample_pos = past_len
    cache_kt = jnp.transpose(cache_k, (0, 2, 3, 1))
    cache_vt = jnp.transpose(cache_v, (0, 2, 3, 1))

    ret_states_p, ret_states_s = [], []
    kv_p = kv_s = None
    for layer in range(DEPTH):
        base = layer * N_MOD
        gain_mix = norm_mix[layer]
        mix_mod_p = (gain_mix, mod_p(base + 0), mod_p(base + 1))
        mix_mod_s = (gain_mix, mod_s(base + 0), mod_s(base + 1))
        if layer < N_A_LAYERS:
            proj_p = _linear(xp, ret_w_in_b[layer], mod=mix_mod_p, out_dtype=BF16, tn=2048,
                             rows_per_group=seq, name="ret_in_prompt")
            gated_p, st_p = _retention_prompt(proj_p, bp, seq)
            xp = _linear(gated_p, ret_w_out_b[layer], epi=(xp, mod_p(base + 2)), rows_per_group=seq,
                         name="ret_out_prompt")
            ret_states_p.append(st_p)
            proj_s = _linear(xs, ret_w_in_b[layer], mod=mix_mod_s, out_dtype=F32, tn=2048, name="ret_in_sample")
            gated_s, st_s = _retention_sample(proj_s, state_ret[layer], sample_pos)
            xs = _linear(gated_s, ret_w_out_b[layer], epi=(xs, mod_s(base + 2)), name="ret_out_sample")
            ret_states_s.append(st_s)
        else:
            j = layer - N_A_LAYERS
            k_p, v_p, fcum_t, k_pb, v_pb = kv_p
            q_p = _linear(xp, fox_w_q_b[j], mod=mix_mod_p, out_dtype=BF16, rows_per_group=seq, name="fox_q_prompt")
            att_p = _fox_prompt(q_p.reshape(bp, seq, d), k_pb, v_pb, fcum_t, bp, seq)
            xp = _linear(att_p.reshape(tp, d), fox_w_o_b[j], epi=(xp, mod_p(base + 2)), rows_per_group=seq,
                         name="fox_o_prompt")
            k_s, v_s, bias_t = kv_s
            q_s = _linear(xs, fox_w_q_b[j], mod=mix_mod_s, out_dtype=F32, name="fox_q_sample")
            att_s = _fox_decode(q_s, k_s, v_s, cache_kt, cache_vt, page_table, bias_t)
            xs = _linear(att_s, fox_w_o_b[j], epi=(xs, mod_s(base + 2)), name="fox_o_sample")

        gain_ffn = norm_ffn[layer]
        buf_rows = _moe_sorted_rows(tp + bs)
        bufs = _router(xp, gain_ffn, mod_p(base + 3), mod_p(base + 4), router_w_pad[layer], router_b_pad[layer],
                       tm=512, total_rows=buf_rows, rows_per_group=seq)
        h_all, ti_all, tw_all = _router(xs, gain_ffn, mod_s(base + 3), mod_s(base + 4), router_w_pad[layer],
                                        router_b_pad[layer], tm=bs, total_rows=buf_rows, row_offset=tp, bufs=bufs)
        xp, xs = _moe_layer([xp, xs], h_all, ti_all, tw_all, [mod_p(base + 5), mod_s(base + 5)], [seq, None],
                            w1[layer], b1[layer], w2[layer], b2[layer])

        if layer == N_A_LAYERS - 1:
            kvb = DEPTH * N_MOD
            kv_mod_p = (norm_kv, mod_p(kvb + 0), mod_p(kvb + 1))
            kv_mod_s = (norm_kv, mod_s(kvb + 0), mod_s(kvb + 1))
            kvp = _linear(xp, kv_w_b, mod=kv_mod_p, rows_per_group=seq, name="kv_prompt")
            lf_p = _linear(xp, f_w_pad, mod=kv_mod_p, bias=f_b_pad, post="log_sigmoid", rows_per_group=seq,
                           name="logf_prompt")[:, :FOX_HEADS]
            k_p, v_p = kvp[:, :d], kvp[:, d:]
            logf_p = lf_p.reshape(bp, seq, FOX_HEADS)
            fcum_t = jnp.transpose(jnp.cumsum(logf_p, axis=1), (0, 2, 1))[:, :, None, :]
            kv_p = (k_p, v_p, fcum_t, k_p.astype(BF16).reshape(bp, seq, d), v_p.astype(BF16).reshape(bp, seq, d))

            kvs = _linear(xs, kv_w_b, mod=kv_mod_s, name="kv_sample")
            lf_s = _linear(xs, f_w_pad, mod=kv_mod_s, bias=f_b_pad, post="log_sigmoid",
                           name="logf_sample")[:, :FOX_HEADS]
            k_s, v_s = kvs[:, :d], kvs[:, d:]
            lf_past = cache_logf[page_table].reshape(bs, past_len, FOX_HEADS).astype(F32)
            cs = jnp.cumsum(lf_past, axis=1)
            bias = (cs[:, -1:, :] - cs) + lf_s[:, None, :]
            kv_s = (k_s, v_s, jnp.transpose(bias, (0, 2, 1)))

    fb = DEPTH * N_MOD + 2
    y_p = _linear(xp, None, mod=(norm_final, mod_p(fb + 0), mod_p(fb + 1)), rows_per_group=seq, name="final_prompt")
    y_s = _linear(xs, None, mod=(norm_final, mod_s(fb + 0), mod_s(fb + 1)), name="final_sample")

    k_p, v_p = kv_p[0], kv_p[1]
    k_s, v_s = kv_s[0], kv_s[1]
    return (y_p.reshape(bp, seq, d),
            y_s.reshape(bs, 1, d),
            jnp.stack(ret_states_p),
            jnp.stack(ret_states_s),
            k_p.reshape(bp, seq, FOX_HEADS, FOX_HEAD_DIM),
            v_p.reshape(bp, seq, FOX_HEADS, FOX_HEAD_DIM),
            logf_p,
            k_s.reshape(bs, 1, FOX_HEADS, FOX_HEAD_DIM),
            v_s.reshape(bs, 1, FOX_HEADS, FOX_HEAD_DIM),
            lf_s.reshape(bs, 1, FOX_HEADS))
```

```python
import functools

import jax
import jax.numpy as jnp
import numpy as np
from jax import lax
from jax.experimental import pallas as pl
from jax.experimental.pallas import tpu as pltpu

F32 = jnp.float32
BF16 = jnp.bfloat16

D_MODEL = 1024
DEPTH = 4
N_A_LAYERS = 2
RET_HEADS = 4
RET_QK_DIM = 256
RET_V_DIM = 512
RET_QK_WIDTH = 1024
RET_V_WIDTH = 2048
RET_CHUNK = 128
ROPE_BASE = 10000.0
FOX_HEADS = 16
FOX_HEAD_DIM = 64
N_EXPERTS = 32
TOP_K = 4
D_FF = 1024
SWIGLU_LIMIT = 7.0
SWIGLU_ALPHA = 1.702
NORM_EPS = 1e-6
N_MOD = 6
MASK_VALUE = -1e30

V7X_VMEM_LIMIT = 56 * 1024 * 1024
LANES = 128


def _cparams(sem):
    return pltpu.CompilerParams(dimension_semantics=sem, vmem_limit_bytes=V7X_VMEM_LIMIT)


def _dot(a, b):
    return jnp.dot(a, b, preferred_element_type=F32)


def _dot_nt(a, b):
    return lax.dot_general(a, b, (((1,), (1,)), ((), ())), preferred_element_type=F32)


def _dot_tn(a, b):
    return lax.dot_general(a, b, (((0,), (0,)), ((), ())), preferred_element_type=F32)


def _modulated(x, gain, shift, scale):
    xf = x.astype(F32)
    y = xf * lax.rsqrt(jnp.mean(xf * xf, axis=-1, keepdims=True) + NORM_EPS)
    return (y * gain) * (1.0 + scale) + shift


def _log_sigmoid(z):
    return jnp.minimum(z, 0.0) - jnp.log1p(jnp.exp(-jnp.abs(z)))


def _linear_body(*refs, mod, pre_silu, has_bias, epi, post, n_j, only_mod):
    it = iter(refs)
    x_ref = next(it)
    if mod:
        gain_ref, shift_ref, scale_ref = next(it), next(it), next(it)
    if not only_mod:
        w_ref = next(it)
    b_ref = next(it) if has_bias else None
    if epi:
        res_ref, gate_ref = next(it), next(it)
    o_ref = next(it)
    h_ref = next(it) if n_j > 1 else None

    def prologue():
        x = x_ref[...]
        if mod:
            return _modulated(x, gain_ref[...], shift_ref[0], scale_ref[0])
        if pre_silu:
            xf = x.astype(F32)
            return xf * jax.nn.sigmoid(xf)
        return x

    if only_mod:
        o_ref[...] = prologue().astype(o_ref.dtype)
        return

    if n_j > 1:
        @pl.when(pl.program_id(1) == 0)
        def _():
            h_ref[...] = prologue().astype(BF16)
        h = h_ref[...]
    else:
        h = prologue().astype(BF16)

    acc = _dot(h, w_ref[...])
    if has_bias:
        acc = acc + b_ref[...]
    if post == "log_sigmoid":
        acc = _log_sigmoid(acc)
    if epi:
        acc = res_ref[...] + gate_ref[0] * acc
    o_ref[...] = acc.astype(o_ref.dtype)


def _linear(x, w=None, *, mod=None, pre_silu=False, bias=None, epi=None, post=None,
            out_dtype=F32, tm=512, tn=1024, rows_per_group=None, name="linear"):
    m, k = x.shape
    only_mod = w is None
    n = k if only_mod else w.shape[1]
    tm = min(tm, m)
    tn = n if only_mod else min(tn, n)
    assert m % tm == 0 and n % tn == 0
    n_i, n_j = m // tm, n // tn
    if only_mod:
        n_j = 1

    def row_or_group(arr, width, col_tiled):
        if arr.ndim == 3:
            tiles_per_group = rows_per_group // tm
            assert rows_per_group % tm == 0
            if col_tiled:
                return arr, pl.BlockSpec((1, 1, tn), lambda i, j: (i // tiles_per_group, 0, j))
            return arr, pl.BlockSpec((1, 1, width), lambda i, j: (i // tiles_per_group, 0, 0))
        arr3 = arr.reshape(n_i, tm, width)
        if col_tiled:
            return arr3, pl.BlockSpec((1, tm, tn), lambda i, j: (i, 0, j))
        return arr3, pl.BlockSpec((1, tm, width), lambda i, j: (i, 0, 0))

    args = [x]
    specs = [pl.BlockSpec((tm, k), lambda i, j: (i, 0))]
    if mod is not None:
        gain, shift, scale = mod
        args.append(gain.reshape(1, k).astype(F32))
        specs.append(pl.BlockSpec((1, k), lambda i, j: (0, 0)))
        for a in (shift, scale):
            a3, sp = row_or_group(a, k, False)
            args.append(a3)
            specs.append(sp)
    if not only_mod:
        args.append(w)
        specs.append(pl.BlockSpec((k, tn), lambda i, j: (0, j)))
    if bias is not None:
        args.append(bias.reshape(1, n).astype(F32))
        specs.append(pl.BlockSpec((1, tn), lambda i, j: (0, j)))
    if epi is not None:
        res, gate = epi
        args.append(res)
        specs.append(pl.BlockSpec((tm, tn), lambda i, j: (i, j)))
        g3, sp = row_or_group(gate, n, True)
        args.append(g3)
        specs.append(sp)

    body = functools.partial(_linear_body, mod=mod is not None, pre_silu=pre_silu,
                             has_bias=bias is not None, epi=epi is not None, post=post,
                             n_j=n_j, only_mod=only_mod)
    scratch = [pltpu.VMEM((tm, k), BF16)] if n_j > 1 else []
    return pl.pallas_call(
        body,
        grid=(n_i, n_j),
        in_specs=specs,
        out_specs=pl.BlockSpec((tm, tn), lambda i, j: (i, j)),
        out_shape=jax.ShapeDtypeStruct((m, n), out_dtype),
        scratch_shapes=scratch,
        compiler_params=_cparams(("parallel", "arbitrary")),
        name=name,
    )(*args)


def _router_body(x_ref, gain_ref, shift_ref, scale_ref, wr_ref, br_ref, h_ref, ti_ref, tw_ref):
    h = _modulated(x_ref[...], gain_ref[...], shift_ref[0], scale_ref[0]).astype(BF16)
    h_ref[...] = h
    logits = _dot(h, wr_ref[...]) + br_ref[...]
    lane = lax.broadcasted_iota(jnp.int32, logits.shape, 1).astype(F32)
    vals, idxs = [], []
    cur = logits
    for _ in range(TOP_K):
        mx = jnp.max(cur, axis=-1, keepdims=True)
        idx = jnp.min(jnp.where(cur == mx, lane, float(LANES)), axis=-1, keepdims=True)
        vals.append(mx)
        idxs.append(idx)
        cur = jnp.where(lane == idx, -jnp.inf, cur)
    exps = [jnp.exp(v - vals[0]) for v in vals]
    denom = exps[0] + exps[1] + exps[2] + exps[3]
    ti = jnp.zeros(logits.shape, F32)
    tw = jnp.zeros(logits.shape, F32)
    for kk in range(TOP_K):
        ti = jnp.where(lane == float(kk), idxs[kk], ti)
        tw = jnp.where(lane == float(kk), exps[kk] / denom, tw)
    ti_ref[...] = ti.astype(jnp.int32)
    tw_ref[...] = tw


def _router(x, gain, shift, scale, wr_pad, br_pad, *, tm, total_rows, row_offset=0, rows_per_group=None,
            bufs=None):
    m, k = x.shape
    tm = min(tm, m)
    n_i = m // tm
    assert row_offset % tm == 0
    off = row_offset // tm
    if shift.ndim == 3:
        tpg = rows_per_group // tm
        mod_spec = pl.BlockSpec((1, 1, k), lambda i: (i // tpg, 0, 0))
        sh3, sc3 = shift, scale
    else:
        mod_spec = pl.BlockSpec((1, tm, k), lambda i: (i, 0, 0))
        sh3, sc3 = shift.reshape(n_i, tm, k), scale.reshape(n_i, tm, k)
    args = [x, gain.reshape(1, k).astype(F32), sh3, sc3, wr_pad, br_pad]
    in_specs = [pl.BlockSpec((tm, k), lambda i: (i, 0)),
                pl.BlockSpec((1, k), lambda i: (0, 0)),
                mod_spec, mod_spec,
                pl.BlockSpec((k, LANES), lambda i: (0, 0)),
                pl.BlockSpec((1, LANES), lambda i: (0, 0))]
    aliases = {}
    body = _router_body
    if bufs is not None:
        aliases = {len(args) + j: j for j in range(3)}
        args += list(bufs)
        in_specs += [pl.BlockSpec(memory_space=pl.ANY)] * 3

        def body(*refs):
            _router_body(*refs[:6], *refs[9:])

    return pl.pallas_call(
        body,
        grid=(n_i,),
        in_specs=in_specs,
        out_specs=[pl.BlockSpec((tm, k), lambda i: (i + off, 0)),
                   pl.BlockSpec((tm, LANES), lambda i: (i + off, 0)),
                   pl.BlockSpec((tm, LANES), lambda i: (i + off, 0))],
        out_shape=[jax.ShapeDtypeStruct((total_rows, k), BF16),
                   jax.ShapeDtypeStruct((total_rows, LANES), jnp.int32),
                   jax.ShapeDtypeStruct((total_rows, LANES), F32)],
        input_output_aliases=aliases,
        compiler_params=_cparams(("parallel",)),
        name="moe_router",
    )(*args)


MOE_TILE = 512
GU_GROUP = 2 * LANES


def _regroup_gu_bias(b):
    lead = b.shape[:-1]
    g = b.reshape(*lead, -1, LANES, 2)
    return jnp.swapaxes(g, -1, -2).reshape(*lead, -1)


def _gmm_body(te_ref, tv_ref, hs_ref, w1_ref, b1_ref, w2_ref, b2_ref, perm_ref, o_ref, w1_s, w2_s):
    i = pl.program_id(0)
    new_expert = (i == 0) | (te_ref[i] != te_ref[jnp.maximum(i - 1, 0)])

    @pl.when(new_expert)
    def _():
        for c in range(2 * D_FF // GU_GROUP):
            cols = slice(c * GU_GROUP, (c + 1) * GU_GROUP)
            w1_s[:, cols] = _dot(w1_ref[0, :, cols].astype(BF16), perm_ref[...]).astype(BF16)
        w2_s[...] = w2_ref[0].astype(BF16)

    @pl.when(tv_ref[i] > 0)
    def _():
        gu = _dot(hs_ref[...], w1_s[...]) + b1_ref[0]
        acts = []
        for c in range(2 * D_FF // GU_GROUP):
            glu = jnp.minimum(gu[:, c * GU_GROUP:c * GU_GROUP + LANES], SWIGLU_LIMIT)
            lin = jnp.clip(gu[:, c * GU_GROUP + LANES:(c + 1) * GU_GROUP], -SWIGLU_LIMIT, SWIGLU_LIMIT)
            acts.append((glu * jax.nn.sigmoid(SWIGLU_ALPHA * glu) * (lin + 1.0)).astype(BF16))
        act = jnp.concatenate(acts, axis=1)
        y = _dot(act, w2_s[...]) + b2_ref[0]
        o_ref[...] = y.astype(o_ref.dtype)

    @pl.when(tv_ref[i] == 0)
    def _():
        o_ref[...] = jnp.zeros_like(o_ref)


def _gu_permutation():
    src = jnp.arange(GU_GROUP)
    dst = (src % 2) * LANES + src // 2
    return jnp.zeros((GU_GROUP, GU_GROUP), BF16).at[src, dst].set(1.0)


def _moe_gmm(tile_expert, tile_valid, tile_block, hs, w1, b1, w2, b2):
    p_rows, d = hs.shape
    n_tiles = p_rows // MOE_TILE
    grid_spec = pltpu.PrefetchScalarGridSpec(
        num_scalar_prefetch=3,
        grid=(n_tiles,),
        in_specs=[
            pl.BlockSpec((MOE_TILE, d), lambda i, te, tv, tb: (tb[i], 0)),
            pl.BlockSpec((1, d, 2 * D_FF), lambda i, te, tv, tb: (te[i], 0, 0)),
            pl.BlockSpec((1, 1, 2 * D_FF), lambda i, te, tv, tb: (te[i], 0, 0)),
            pl.BlockSpec((1, D_FF, d), lambda i, te, tv, tb: (te[i], 0, 0)),
            pl.BlockSpec((1, 1, d), lambda i, te, tv, tb: (te[i], 0, 0)),
            pl.BlockSpec((GU_GROUP, GU_GROUP), lambda i, te, tv, tb: (0, 0)),
        ],
        out_specs=pl.BlockSpec((MOE_TILE, d), lambda i, te, tv, tb: (i, 0)),
        scratch_shapes=[pltpu.VMEM((d, 2 * D_FF), BF16), pltpu.VMEM((D_FF, d), BF16)],
    )

    def body(te_ref, tv_ref, tb_ref, *rest):
        _gmm_body(te_ref, tv_ref, *rest)

    return pl.pallas_call(
        body,
        grid_spec=grid_spec,
        out_shape=jax.ShapeDtypeStruct((p_rows, d), BF16),
        compiler_params=_cparams(("arbitrary",)),
        name="moe_gmm",
    )(tile_expert, tile_valid, tile_block, hs, w1, b1, w2, b2, _gu_permutation())


def _combine_body(x_ref, y_ref, tw_ref, g_ref, o_ref):
    w = tw_ref[...]
    acc = y_ref[0].astype(F32) * w[:, 0:1]
    for kk in range(1, TOP_K):
        acc = acc + y_ref[kk].astype(F32) * w[:, kk:kk + 1]
    o_ref[...] = x_ref[...] + g_ref[0] * acc


def _moe_combine(x, yk, tw_all, gate, *, tm, row_offset, rows_per_group=None):
    m, d = x.shape
    tm = min(tm, m)
    n_i = m // tm
    off = row_offset // tm
    if gate.ndim == 3:
        tpg = rows_per_group // tm
        g3, g_spec = gate, pl.BlockSpec((1, 1, d), lambda i: (i // tpg, 0, 0))
    else:
        g3, g_spec = gate.reshape(n_i, tm, d), pl.BlockSpec((1, tm, d), lambda i: (i, 0, 0))
    return pl.pallas_call(
        _combine_body,
        grid=(n_i,),
        in_specs=[pl.BlockSpec((tm, d), lambda i: (i, 0)),
                  pl.BlockSpec((TOP_K, tm, d), lambda i: (0, i + off, 0)),
                  pl.BlockSpec((tm, LANES), lambda i: (i + off, 0)),
                  g_spec],
        out_specs=pl.BlockSpec((tm, d), lambda i: (i, 0)),
        out_shape=jax.ShapeDtypeStruct((m, d), F32),
        compiler_params=_cparams(("parallel",)),
        name="moe_combine",
    )(x, yk, tw_all, g3)


def _moe_sorted_rows(t):
    return ((t * TOP_K + N_EXPERTS * (MOE_TILE - 1)) // MOE_TILE + 1) * MOE_TILE


def _moe_layer(x_parts, h_all, ti_all, tw_all, gates2, rows_per_group, expert_base, w1, b1, w2, b2):
    t = sum(xp.shape[0] for xp in x_parts)
    n_slots = t * TOP_K
    flat_e = ti_all[:t, :TOP_K].T.reshape(n_slots)
    onehot = (flat_e[:, None] == jnp.arange(N_EXPERTS, dtype=jnp.int32)[None, :]).astype(jnp.int32)
    csum = jnp.cumsum(onehot, axis=0)
    counts = csum[-1]
    rank = jnp.sum(csum * onehot, axis=1) - 1
    padded = ((counts + MOE_TILE - 1) // MOE_TILE) * MOE_TILE
    poffs = jnp.cumsum(padded) - padded
    total_padded = jnp.sum(padded)
    p_rows = _moe_sorted_rows(t)
    n_tiles = p_rows // MOE_TILE
    tile_start = jnp.arange(n_tiles, dtype=jnp.int32) * MOE_TILE
    tile_valid = (tile_start < total_padded).astype(jnp.int32)
    pends = poffs + padded
    tile_expert = jnp.minimum(jnp.sum((pends[None, :] <= tile_start[:, None]).astype(jnp.int32), axis=1),
                              N_EXPERTS - 1)
    last_valid = jnp.maximum(total_padded // MOE_TILE - 1, 0).astype(jnp.int32)
    tile_block = jnp.minimum(jnp.arange(n_tiles, dtype=jnp.int32), last_valid)
    tile_expert = jnp.where(tile_valid > 0, tile_expert, tile_expert[last_valid])
    inv = poffs[flat_e] + rank
    row_token = (jnp.arange(p_rows, dtype=jnp.int32) % t).at[inv].set(jnp.arange(n_slots, dtype=jnp.int32) % t)

    hs = jnp.take(h_all, row_token, axis=0)
    ys = _moe_gmm(tile_expert + expert_base, tile_valid, tile_block, hs, w1, b1, w2, b2)
    yk = jnp.take(ys, inv, axis=0).reshape(TOP_K, t, -1)
    outs, start = [], 0
    for xp, g2, rpg in zip(x_parts, gates2, rows_per_group):
        r = xp.shape[0]
        outs.append(_moe_combine(xp, yk, tw_all, g2, tm=512, row_offset=start, rows_per_group=rpg))
        start += r
    return outs


def _rope_tables(pos):
    half = RET_QK_DIM // 2
    inv = 1.0 / (ROPE_BASE ** jnp.linspace(0.0, 1.0, half, dtype=F32))
    ang = jnp.repeat(pos.astype(F32)[:, None] * inv[None, :], 2, axis=-1)
    cos, sin = jnp.cos(ang), jnp.sin(ang)
    even = (jnp.arange(RET_QK_DIM) % 2 == 0)[None, :]
    return cos, jnp.where(even, -sin, 0.0), jnp.where(even, 0.0, sin)


def _rope(x, cos, sin_next, sin_prev):
    n = x.shape[-1]
    return x * cos + pltpu.roll(x, n - 1, 1) * sin_next + pltpu.roll(x, 1, 1) * sin_prev


def _log_gamma():
    return jnp.log1p(-jnp.exp2(-5.0 - jnp.arange(RET_HEADS, dtype=F32)))


def _ret_prompt_body(q_ref, k_ref, v_ref, g_ref, cos_ref, sn_ref, sp_ref, dec_ref, rd_ref,
                     o_ref, st_ref, state):
    c = pl.program_id(2)

    @pl.when(c == 0)
    def _():
        state[...] = jnp.zeros_like(state)

    cos, sn, sp = cos_ref[...], sn_ref[...], sp_ref[...]
    q = _rope(q_ref[0].astype(F32), cos, sn, sp)
    k = _rope(k_ref[0].astype(F32), cos, sn, sp) * (RET_QK_DIM ** -0.5)
    v = v_ref[0]
    rd = rd_ref[0]
    q_dec = (q * rd[:, 0:1]).astype(BF16)
    k_dec = (k * rd[:, 1:2]).astype(BF16)
    inner = (_dot_nt(q.astype(BF16), k.astype(BF16)) * dec_ref[0]).astype(BF16)
    s_old = state[...]
    out = _dot(inner, v) + _dot(q_dec, s_old.astype(BF16))
    state[...] = rd[0:1, 2:3] * s_old + _dot_tn(k_dec, v)
    out = out * lax.rsqrt(jnp.mean(out * out, axis=-1, keepdims=True) + NORM_EPS)
    g = g_ref[0].astype(F32)
    o_ref[0] = (g * jax.nn.sigmoid(g) * out).astype(o_ref.dtype)

    @pl.when(c == pl.num_programs(2) - 1)
    def _():
        st_ref[0, 0] = state[...]


def _retention_prompt(proj, batch, seq):
    c = RET_CHUNK
    n_chunks = seq // c
    proj3 = proj.reshape(batch, seq, -1)
    cos, sn, sp = _rope_tables(jnp.arange(seq, dtype=jnp.int32))
    lg = _log_gamma()
    idx = jnp.arange(c, dtype=F32)
    diff = idx[:, None] - idx[None, :]
    decay = jnp.where(diff >= 0, jnp.exp(jnp.maximum(diff, 0.0)[None] * lg[:, None, None]), 0.0)
    qd = jnp.exp((idx + 1.0)[None, :] * lg[:, None])
    kd = jnp.exp((c - 1.0 - idx)[None, :] * lg[:, None])
    cd = jnp.broadcast_to(jnp.exp(c * lg)[:, None], (RET_HEADS, c))
    rd = jnp.zeros((RET_HEADS, c, LANES), F32).at[:, :, 0].set(qd).at[:, :, 1].set(kd).at[:, :, 2].set(cd)
    nqk = RET_QK_WIDTH // RET_QK_DIM
    out, st = pl.pallas_call(
        _ret_prompt_body,
        grid=(batch, RET_HEADS, n_chunks),
        in_specs=[
            pl.BlockSpec((1, c, RET_QK_DIM), lambda b, h, i: (b, i, h)),
            pl.BlockSpec((1, c, RET_QK_DIM), lambda b, h, i: (b, i, nqk + h)),
            pl.BlockSpec((1, c, RET_V_DIM), lambda b, h, i: (b, i, nqk + h)),
            pl.BlockSpec((1, c, RET_V_DIM), lambda b, h, i: (b, i, 2 * nqk + h)),
            pl.BlockSpec((c, RET_QK_DIM), lambda b, h, i: (i, 0)),
            pl.BlockSpec((c, RET_QK_DIM), lambda b, h, i: (i, 0)),
            pl.BlockSpec((c, RET_QK_DIM), lambda b, h, i: (i, 0)),
            pl.BlockSpec((1, c, c), lambda b, h, i: (h, 0, 0)),
            pl.BlockSpec((1, c, LANES), lambda b, h, i: (h, 0, 0)),
        ],
        out_specs=[
            pl.BlockSpec((1, c, RET_V_DIM), lambda b, h, i: (b, i, h)),
            pl.BlockSpec((1, 1, RET_QK_DIM, RET_V_DIM), lambda b, h, i: (b, h, 0, 0)),
        ],
        out_shape=[jax.ShapeDtypeStruct((batch, seq, RET_V_WIDTH), BF16),
                   jax.ShapeDtypeStruct((batch, RET_HEADS, RET_QK_DIM, RET_V_DIM), F32)],
        scratch_shapes=[pltpu.VMEM((RET_QK_DIM, RET_V_DIM), F32)],
        compiler_params=_cparams(("parallel", "parallel", "arbitrary")),
        name="retention_prompt",
    )(proj3, proj3, proj3, proj3, cos, sn, sp, decay, rd)
    return out.reshape(batch * seq, RET_V_WIDTH), st


def _ret_sample_body(p_ref, st_ref, rope_ref, gam_ref, o_ref, ns_ref):
    cos, sn, sp = rope_ref[0:1, :], rope_ref[1:2, :], rope_ref[2:3, :]
    row0 = lax.broadcasted_iota(jnp.int32, (8, RET_QK_DIM), 0) == 0
    for h in range(RET_HEADS):
        gamma = gam_ref[h:h + 1, 0:1]
        q = p_ref[0, :, h * RET_QK_DIM:(h + 1) * RET_QK_DIM].astype(F32)
        k = p_ref[0, :, RET_QK_WIDTH + h * RET_QK_DIM:RET_QK_WIDTH + (h + 1) * RET_QK_DIM].astype(F32)
        v0 = 2 * RET_QK_WIDTH + h * RET_V_DIM
        v = p_ref[0, :, v0:v0 + RET_V_DIM].astype(F32)
        g0 = 2 * RET_QK_WIDTH + RET_V_WIDTH + h * RET_V_DIM
        g = p_ref[0, :, g0:g0 + RET_V_DIM].astype(F32)
        q = _rope(jnp.broadcast_to(q, (8, RET_QK_DIM)), cos, sn, sp)
        k = _rope(jnp.broadcast_to(k, (8, RET_QK_DIM)), cos, sn, sp) * (RET_QK_DIM ** -0.5)
        qb = q.astype(BF16)
        kb = k.astype(BF16)
        vb = v.astype(BF16)
        s_old = st_ref[0, 0, h]
        qk = jnp.sum(qb.astype(F32) * kb.astype(F32), axis=-1, keepdims=True)
        out = (qk.astype(BF16).astype(F32) * vb.astype(F32)
               + _dot((q * gamma).astype(BF16), s_old.astype(BF16)))
        k_row0 = jnp.where(row0, kb, jnp.zeros_like(kb))
        v8 = jnp.broadcast_to(vb, (8, RET_V_DIM))
        ns_ref[0, 0, h] = gamma * s_old + _dot_tn(k_row0, v8)
        out = out[0:1]
        out = out * lax.rsqrt(jnp.mean(out * out, axis=-1, keepdims=True) + NORM_EPS)
        o_ref[0, :, h * RET_V_DIM:(h + 1) * RET_V_DIM] = (g * jax.nn.sigmoid(g) * out).astype(o_ref.dtype)


def _retention_sample(proj, states, layer, pos, new_states=None):
    b = proj.shape[0]
    cos, sn, sp = _rope_tables(jnp.full((1,), pos, jnp.int32))
    rope = jnp.zeros((8, RET_QK_DIM), F32).at[0].set(cos[0]).at[1].set(sn[0]).at[2].set(sp[0])
    gam = jnp.broadcast_to(jnp.exp(_log_gamma())[:, None], (RET_HEADS, LANES))
    width = proj.shape[1]
    state_spec = pl.BlockSpec((1, 1, RET_HEADS, RET_QK_DIM, RET_V_DIM), lambda i: (layer, i, 0, 0, 0))
    args = [proj.reshape(b, 1, width), states, rope, gam]
    in_specs = [pl.BlockSpec((1, 1, width), lambda i: (i, 0, 0)),
                state_spec,
                pl.BlockSpec((8, RET_QK_DIM), lambda i: (0, 0)),
                pl.BlockSpec((RET_HEADS, LANES), lambda i: (0, 0))]
    aliases = {}
    body = _ret_sample_body
    if new_states is not None:
        aliases = {len(args): 1}
        args.append(new_states)
        in_specs.append(pl.BlockSpec(memory_space=pl.ANY))

        def body(p_ref, st_ref, rope_ref, gam_ref, _prev, o_ref, ns_ref):
            _ret_sample_body(p_ref, st_ref, rope_ref, gam_ref, o_ref, ns_ref)

    out, ns = pl.pallas_call(
        body,
        grid=(b,),
        in_specs=in_specs,
        out_specs=[pl.BlockSpec((1, 1, RET_V_WIDTH), lambda i: (i, 0, 0)), state_spec],
        out_shape=[jax.ShapeDtypeStruct((b, 1, RET_V_WIDTH), BF16),
                   jax.ShapeDtypeStruct(states.shape, states.dtype)],
        input_output_aliases=aliases,
        compiler_params=_cparams(("parallel",)),
        name="retention_sample",
    )(*args)
    return out.reshape(b, RET_V_WIDTH), ns


FOX_TQ = 1024


def _fox_prompt_body(qi_ref, ki_ref, q_ref, k_ref, v_ref, fk_ref, o_ref, qa_s, m_s, acc_s):
    p = pl.program_id(2)
    qi, ki = qi_ref[p], ki_ref[p]
    tq = FOX_TQ
    lane = lax.broadcasted_iota(jnp.int32, (tq, LANES), 1)

    def head_lanes(hh):
        return (lane >= hh * FOX_HEAD_DIM) & (lane < (hh + 1) * FOX_HEAD_DIM)

    @pl.when(ki == 0)
    def _():
        q2 = q_ref[0].astype(F32) * (FOX_HEAD_DIM ** -0.5)
        for hh in range(2):
            qa_s[hh] = jnp.where(head_lanes(hh), q2, 0.0).astype(BF16)
        m_s[...] = jnp.full_like(m_s, MASK_VALUE)
        acc_s[...] = jnp.zeros_like(acc_s)

    def step(masked):
        k2 = k_ref[0]
        v2 = v_ref[0].astype(F32)
        if masked:
            row = lax.broadcasted_iota(jnp.int32, (tq, tq), 0)
            col = lax.broadcasted_iota(jnp.int32, (tq, tq), 1)
            keep = col <= row
        for hh in range(2):
            v_aug = jnp.where(head_lanes(hh), v2, 1.0).astype(BF16)
            s = _dot_nt(qa_s[hh], k2) - fk_ref[0, hh]
            if masked:
                s = jnp.where(keep, s, MASK_VALUE)
            m_prev = m_s[hh]
            m_next = jnp.maximum(m_prev, jnp.max(s, axis=1, keepdims=True))
            alpha = jnp.exp(m_prev - m_next)
            pr = jnp.exp(s - jnp.tile(m_next, (1, tq // LANES)))
            acc_s[hh] = alpha * acc_s[hh] + _dot(pr.astype(BF16), v_aug)
            m_s[hh] = m_next

    @pl.when(ki < qi)
    def _():
        step(False)

    @pl.when(ki == qi)
    def _():
        step(True)
        a0, a1 = acc_s[0], acc_s[1]
        o0 = a0 / pltpu.roll(a0, FOX_HEAD_DIM, 1)
        o1 = a1 / pltpu.roll(a1, FOX_HEAD_DIM, 1)
        o_ref[0] = jnp.where(lane < FOX_HEAD_DIM, o0, o1).astype(o_ref.dtype)


def _fox_prompt(q, k, v, fcum_t, batch, seq):
    tq = FOX_TQ
    n_q = seq // tq
    pairs = [(i, j) for i in range(n_q) for j in range(i + 1)]
    qi_tab = jnp.asarray([p[0] for p in pairs], jnp.int32)
    ki_tab = jnp.asarray([p[1] for p in pairs], jnp.int32)
    n_hp = FOX_HEADS // 2
    grid_spec = pltpu.PrefetchScalarGridSpec(
        num_scalar_prefetch=2,
        grid=(batch, n_hp, len(pairs)),
        in_specs=[
            pl.BlockSpec((1, tq, LANES), lambda b, h, p, qi, ki: (b, qi[p], h)),
            pl.BlockSpec((1, tq, LANES), lambda b, h, p, qi, ki: (b, ki[p], h)),
            pl.BlockSpec((1, tq, LANES), lambda b, h, p, qi, ki: (b, ki[p], h)),
            pl.BlockSpec((1, 2, 1, tq), lambda b, h, p, qi, ki: (b, h, 0, ki[p])),
        ],
        out_specs=pl.BlockSpec((1, tq, LANES), lambda b, h, p, qi, ki: (b, qi[p], h)),
        scratch_shapes=[pltpu.VMEM((2, tq, LANES), BF16), pltpu.VMEM((2, tq, LANES), F32),
                        pltpu.VMEM((2, tq, LANES), F32)],
    )
    return pl.pallas_call(
        _fox_prompt_body,
        grid_spec=grid_spec,
        out_shape=jax.ShapeDtypeStruct((batch, seq, FOX_HEADS * FOX_HEAD_DIM), BF16),
        compiler_params=_cparams(("parallel", "parallel", "arbitrary")),
        name="fox_prompt",
    )(qi_tab, ki_tab, q, k, v, fcum_t)


DEC_PAGES = 8


def _fox_decode_body(pt_ref, q_ref, kn_ref, vn_ref, bias_ref, *rest):
    k_refs, v_refs = rest[:DEC_PAGES], rest[DEC_PAGES:2 * DEC_PAGES]
    o_ref, m_s, l_s, acc_s = rest[2 * DEC_PAGES:]
    g = pl.program_id(1)
    width = FOX_HEADS * FOX_HEAD_DIM
    page = k_refs[0].shape[-1]
    lane = lax.broadcasted_iota(jnp.int32, (FOX_HEADS, width), 1)
    row = lax.broadcasted_iota(jnp.int32, (FOX_HEADS, width), 0)
    own = (lane >= row * FOX_HEAD_DIM) & (lane < (row + 1) * FOX_HEAD_DIM)
    scale = FOX_HEAD_DIM ** -0.5
    q_row = jnp.broadcast_to(q_ref[0].astype(F32), (FOX_HEADS, width))
    q_blk = jnp.where(own, q_row, 0.0).astype(BF16)

    @pl.when(g == 0)
    def _():
        m_s[...] = jnp.full_like(m_s, MASK_VALUE)
        l_s[...] = jnp.zeros_like(l_s)
        acc_s[...] = jnp.zeros_like(acc_s)

    k_t = jnp.concatenate([r[0].reshape(width, page).astype(BF16) for r in k_refs], axis=1)
    s = _dot(q_blk, k_t) * scale + bias_ref[0]
    m_prev = m_s[...]
    m_next = jnp.maximum(m_prev, jnp.max(s, axis=-1, keepdims=True))
    alpha = jnp.exp(m_prev - m_next)
    pr = jnp.exp(s - m_next)
    l_s[...] = alpha * l_s[...] + jnp.sum(pr, axis=-1, keepdims=True)
    prb = pr.astype(BF16)
    pv = _dot_nt(prb[:, 0:page], v_refs[0][0].reshape(width, page).astype(BF16))
    for i in range(1, DEC_PAGES):
        pv = pv + _dot_nt(prb[:, i * page:(i + 1) * page], v_refs[i][0].reshape(width, page).astype(BF16))
    acc_s[...] = alpha * acc_s[...] + pv
    m_s[...] = m_next

    @pl.when(g == pl.num_programs(1) - 1)
    def _():
        knb = kn_ref[0].astype(BF16).astype(F32)
        s_new = jnp.sum(q_blk.astype(F32) * knb, axis=-1, keepdims=True) * scale
        m_fin = jnp.maximum(m_next, s_new)
        a = jnp.exp(m_next - m_fin)
        p_new = jnp.exp(s_new - m_fin)
        denom = a * l_s[...] + p_new
        vnb = vn_ref[0].astype(BF16).astype(F32)
        o_full = (a * acc_s[...] + p_new.astype(BF16).astype(F32) * vnb) / denom
        o_ref[0] = jnp.sum(jnp.where(own, o_full, 0.0), axis=0, keepdims=True).astype(o_ref.dtype)


def _fox_decode(q, k_new, v_new, cache_kt, cache_vt, page_table, bias_t):
    b, width = q.shape
    n_pages = page_table.shape[1]
    _, heads, hd, page = cache_kt.shape
    n_groups = n_pages // DEC_PAGES
    vec = pl.BlockSpec((1, 1, width), lambda i, g, pt: (i, 0, 0))

    def page_spec(j):
        return pl.BlockSpec((1, heads, hd, page), lambda i, g, pt: (pt[i, g * DEC_PAGES + j], 0, 0, 0))

    grid_spec = pltpu.PrefetchScalarGridSpec(
        num_scalar_prefetch=1,
        grid=(b, n_groups),
        in_specs=[vec, vec, vec,
                  pl.BlockSpec((1, FOX_HEADS, DEC_PAGES * page), lambda i, g, pt: (i, 0, g))]
                 + [page_spec(j) for j in range(DEC_PAGES)] * 2,
        out_specs=vec,
        scratch_shapes=[pltpu.VMEM((FOX_HEADS, 1), F32), pltpu.VMEM((FOX_HEADS, 1), F32),
                        pltpu.VMEM((FOX_HEADS, width), F32)],
    )
    out = pl.pallas_call(
        _fox_decode_body,
        grid_spec=grid_spec,
        out_shape=jax.ShapeDtypeStruct((b, 1, width), BF16),
        compiler_params=_cparams(("parallel", "arbitrary")),
        name="fox_decode",
    )(page_table, q.reshape(b, 1, width), k_new.reshape(b, 1, width), v_new.reshape(b, 1, width), bias_t,
      *([cache_kt] * DEC_PAGES), *([cache_vt] * DEC_PAGES))
    return out.reshape(b, width)


def kernel(x_prompt, x_sample, state_ret, cache_k, cache_v, cache_logf, page_table, c_prompt, c_sample,
           norm_mix, norm_ffn, ada_w, ada_b, ret_w_in, ret_w_out, fox_w_q, fox_w_o, norm_kv, ada_kv_w,
           ada_kv_b, kv_w, f_w, f_b, router_w, router_b, expert_w_gu, expert_b_gu, expert_w_down,
           expert_b_down, norm_final, ada_final_w, ada_final_b):
    d = D_MODEL
    bp, seq, _ = x_prompt.shape
    bs = x_sample.shape[0]
    tp = bp * seq
    past_len = page_table.shape[1] * cache_k.shape[1]

    ret_w_in_b = ret_w_in.astype(BF16)
    ret_w_out_b = ret_w_out.astype(BF16)
    fox_w_q_b = fox_w_q.astype(BF16)
    fox_w_o_b = fox_w_o.astype(BF16)
    kv_w_b = kv_w.astype(BF16)
    f_w_pad = jnp.zeros((d, LANES), BF16).at[:, :FOX_HEADS].set(f_w.astype(BF16))
    f_b_pad = jnp.zeros((LANES,), F32).at[:FOX_HEADS].set(f_b)
    router_w_pad = jnp.zeros((DEPTH, d, LANES), BF16).at[:, :, :N_EXPERTS].set(router_w.astype(BF16))
    router_b_pad = jnp.full((DEPTH, 1, LANES), -jnp.inf, F32).at[:, 0, :N_EXPERTS].set(router_b)
    w1 = expert_w_gu.reshape(DEPTH * N_EXPERTS, d, 2 * D_FF)
    b1 = _regroup_gu_bias(expert_b_gu).reshape(DEPTH * N_EXPERTS, 1, 2 * D_FF)
    w2 = expert_w_down.reshape(DEPTH * N_EXPERTS, D_FF, d)
    b2 = expert_b_down.reshape(DEPTH * N_EXPERTS, 1, d)

    ada_all_w = jnp.concatenate([ada_w[l] for l in range(DEPTH)] + [ada_kv_w, ada_final_w], axis=1).astype(BF16)
    ada_all_b = jnp.concatenate([ada_b[l] for l in range(DEPTH)] + [ada_kv_b, ada_final_b], axis=0)
    n_c = bs + bp
    n_c_pad = ((n_c + 7) // 8) * 8
    c_all = jnp.zeros((n_c_pad, d), F32).at[:bs].set(c_sample).at[bs:n_c].set(c_prompt)
    mods = _linear(c_all, ada_all_w, pre_silu=True, bias=ada_all_b, tm=n_c_pad, tn=2048, name="ada_params")

    def mod_s(col):
        return mods[:bs, col * d:(col + 1) * d]

    def mod_p(col):
        return mods[bs:n_c, col * d:(col + 1) * d].reshape(bp, 1, d)

    xp = x_prompt.reshape(tp, d)
    xs = x_sample.reshape(bs, d)
    sample_pos = past_len
    cache_kt = jnp.transpose(cache_k, (0, 2, 3, 1))
    cache_vt = jnp.transpose(cache_v, (0, 2, 3, 1))

    ret_states_p, ret_states_s = [], None
    kv_p = kv_s = None
    for layer in range(DEPTH):
        base = layer * N_MOD
        gain_mix = norm_mix[layer]
        mix_mod_p = (gain_mix, mod_p(base + 0), mod_p(base + 1))
        mix_mod_s = (gain_mix, mod_s(base + 0), mod_s(base + 1))
        if layer < N_A_LAYERS:
            proj_p = _linear(xp, ret_w_in_b[layer], mod=mix_mod_p, out_dtype=BF16, tn=2048,
                             rows_per_group=seq, name="ret_in_prompt")
            gated_p, st_p = _retention_prompt(proj_p, bp, seq)
            xp = _linear(gated_p, ret_w_out_b[layer], epi=(xp, mod_p(base + 2)), rows_per_group=seq,
                         name="ret_out_prompt")
            ret_states_p.append(st_p)
            proj_s = _linear(xs, ret_w_in_b[layer], mod=mix_mod_s, out_dtype=F32, tn=2048, name="ret_in_sample")
            gated_s, ret_states_s = _retention_sample(proj_s, state_ret, layer, sample_pos, ret_states_s)
            xs = _linear(gated_s, ret_w_out_b[layer], epi=(xs, mod_s(base + 2)), name="ret_out_sample")
        else:
            j = layer - N_A_LAYERS
            k_p, v_p, fcum_t, k_pb, v_pb = kv_p
            q_p = _linear(xp, fox_w_q_b[j], mod=mix_mod_p, out_dtype=BF16, rows_per_group=seq, name="fox_q_prompt")
            att_p = _fox_prompt(q_p.reshape(bp, seq, d), k_pb, v_pb, fcum_t, bp, seq)
            xp = _linear(att_p.reshape(tp, d), fox_w_o_b[j], epi=(xp, mod_p(base + 2)), rows_per_group=seq,
                         name="fox_o_prompt")
            k_s, v_s, bias_t = kv_s
            q_s = _linear(xs, fox_w_q_b[j], mod=mix_mod_s, out_dtype=F32, name="fox_q_sample")
            att_s = _fox_decode(q_s, k_s, v_s, cache_kt, cache_vt, page_table, bias_t)
            xs = _linear(att_s, fox_w_o_b[j], epi=(xs, mod_s(base + 2)), name="fox_o_sample")

        gain_ffn = norm_ffn[layer]
        buf_rows = _moe_sorted_rows(tp + bs)
        bufs = _router(xp, gain_ffn, mod_p(base + 3), mod_p(base + 4), router_w_pad[layer], router_b_pad[layer],
                       tm=512, total_rows=buf_rows, rows_per_group=seq)
        h_all, ti_all, tw_all = _router(xs, gain_ffn, mod_s(base + 3), mod_s(base + 4), router_w_pad[layer],
                                        router_b_pad[layer], tm=bs, total_rows=buf_rows, row_offset=tp, bufs=bufs)
        xp, xs = _moe_layer([xp, xs], h_all, ti_all, tw_all, [mod_p(base + 5), mod_s(base + 5)], [seq, None],
                            layer * N_EXPERTS, w1, b1, w2, b2)

        if layer == N_A_LAYERS - 1:
            kvb = DEPTH * N_MOD
            kv_mod_p = (norm_kv, mod_p(kvb + 0), mod_p(kvb + 1))
            kv_mod_s = (norm_kv, mod_s(kvb + 0), mod_s(kvb + 1))
            kvp = _linear(xp, kv_w_b, mod=kv_mod_p, rows_per_group=seq, name="kv_prompt")
            lf_p = _linear(xp, f_w_pad, mod=kv_mod_p, bias=f_b_pad, post="log_sigmoid", rows_per_group=seq,
                           name="logf_prompt")[:, :FOX_HEADS]
            k_p, v_p = kvp[:, :d], kvp[:, d:]
            logf_p = lf_p.reshape(bp, seq, FOX_HEADS)
            fcum_t = jnp.transpose(jnp.cumsum(logf_p, axis=1), (0, 2, 1))[:, :, None, :]
            kv_p = (k_p, v_p, fcum_t, k_p.astype(BF16).reshape(bp, seq, d), v_p.astype(BF16).reshape(bp, seq, d))

            kvs = _linear(xs, kv_w_b, mod=kv_mod_s, name="kv_sample")
            lf_s = _linear(xs, f_w_pad, mod=kv_mod_s, bias=f_b_pad, post="log_sigmoid",
                           name="logf_sample")[:, :FOX_HEADS]
            k_s, v_s = kvs[:, :d], kvs[:, d:]
            lf_past = cache_logf[page_table].reshape(bs, past_len, FOX_HEADS).astype(F32)
            cs = jnp.cumsum(lf_past, axis=1)
            bias = (cs[:, -1:, :] - cs) + lf_s[:, None, :]
            kv_s = (k_s, v_s, jnp.transpose(bias, (0, 2, 1)))

    fb = DEPTH * N_MOD + 2
    y_p = _linear(xp, None, mod=(norm_final, mod_p(fb + 0), mod_p(fb + 1)), rows_per_group=seq, name="final_prompt")
    y_s = _linear(xs, None, mod=(norm_final, mod_s(fb + 0), mod_s(fb + 1)), name="final_sample")

    k_p, v_p = kv_p[0], kv_p[1]
    k_s, v_s = kv_s[0], kv_s[1]
    return (y_p.reshape(bp, seq, d),
            y_s.reshape(bs, 1, d),
            jnp.stack(ret_states_p),
            ret_states_s,
            k_p.reshape(bp, seq, FOX_HEADS, FOX_HEAD_DIM),
            v_p.reshape(bp, seq, FOX_HEADS, FOX_HEAD_DIM),
            logf_p,
            k_s.reshape(bs, 1, FOX_HEADS, FOX_HEAD_DIM),
            v_s.reshape(bs, 1, FOX_HEADS, FOX_HEAD_DIM),
            lf_s.reshape(bs, 1, FOX_HEADS))
```

```python
import functools

import jax
import jax.numpy as jnp
import numpy as np
from jax import lax
from jax.experimental import pallas as pl
from jax.experimental.pallas import tpu as pltpu

F32 = jnp.float32
BF16 = jnp.bfloat16

D_MODEL = 1024
DEPTH = 4
N_A_LAYERS = 2
RET_HEADS = 4
RET_QK_DIM = 256
RET_V_DIM = 512
RET_QK_WIDTH = 1024
RET_V_WIDTH = 2048
RET_CHUNK = 128
ROPE_BASE = 10000.0
FOX_HEADS = 16
FOX_HEAD_DIM = 64
N_EXPERTS = 32
TOP_K = 4
D_FF = 1024
SWIGLU_LIMIT = 7.0
SWIGLU_ALPHA = 1.702
NORM_EPS = 1e-6
N_MOD = 6
MASK_VALUE = -1e30

V7X_VMEM_LIMIT = 56 * 1024 * 1024
LANES = 128


def _cparams(sem):
    return pltpu.CompilerParams(dimension_semantics=sem, vmem_limit_bytes=V7X_VMEM_LIMIT)


def _dot(a, b):
    return jnp.dot(a, b, preferred_element_type=F32)


def _dot_nt(a, b):
    return lax.dot_general(a, b, (((1,), (1,)), ((), ())), preferred_element_type=F32)


def _dot_tn(a, b):
    return lax.dot_general(a, b, (((0,), (0,)), ((), ())), preferred_element_type=F32)


def _modulated(x, gain, shift, scale):
    xf = x.astype(F32)
    y = xf * lax.rsqrt(jnp.mean(xf * xf, axis=-1, keepdims=True) + NORM_EPS)
    return (y * gain) * (1.0 + scale) + shift


def _log_sigmoid(z):
    return jnp.minimum(z, 0.0) - jnp.log1p(jnp.exp(-jnp.abs(z)))


def _linear_body(*refs, mod, pre_silu, has_bias, epi, post, n_j, only_mod):
    it = iter(refs)
    x_ref = next(it)
    if mod:
        gain_ref, shift_ref, scale_ref = next(it), next(it), next(it)
    if not only_mod:
        w_ref = next(it)
    b_ref = next(it) if has_bias else None
    if epi:
        res_ref, gate_ref = next(it), next(it)
    o_ref = next(it)
    h_ref = next(it) if n_j > 1 else None

    def prologue():
        x = x_ref[...]
        if mod:
            return _modulated(x, gain_ref[...], shift_ref[0], scale_ref[0])
        if pre_silu:
            xf = x.astype(F32)
            return xf * jax.nn.sigmoid(xf)
        return x

    if only_mod:
        o_ref[...] = prologue().astype(o_ref.dtype)
        return

    if n_j > 1:
        @pl.when(pl.program_id(1) == 0)
        def _():
            h_ref[...] = prologue().astype(BF16)
        h = h_ref[...]
    else:
        h = prologue().astype(BF16)

    acc = _dot(h, w_ref[...])
    if has_bias:
        acc = acc + b_ref[...]
    if post == "log_sigmoid":
        acc = _log_sigmoid(acc)
    if epi:
        acc = res_ref[...] + gate_ref[0] * acc
    o_ref[...] = acc.astype(o_ref.dtype)


def _linear(x, w=None, *, mod=None, pre_silu=False, bias=None, epi=None, post=None,
            out_dtype=F32, tm=512, tn=1024, rows_per_group=None, name="linear"):
    m, k = x.shape
    only_mod = w is None
    n = k if only_mod else w.shape[1]
    tm = min(tm, m)
    tn = n if only_mod else min(tn, n)
    assert m % tm == 0 and n % tn == 0
    n_i, n_j = m // tm, n // tn
    if only_mod:
        n_j = 1

    def row_or_group(arr, width, col_tiled):
        if arr.ndim == 3:
            tiles_per_group = rows_per_group // tm
            assert rows_per_group % tm == 0
            if col_tiled:
                return arr, pl.BlockSpec((1, 1, tn), lambda i, j: (i // tiles_per_group, 0, j))
            return arr, pl.BlockSpec((1, 1, width), lambda i, j: (i // tiles_per_group, 0, 0))
        arr3 = arr.reshape(n_i, tm, width)
        if col_tiled:
            return arr3, pl.BlockSpec((1, tm, tn), lambda i, j: (i, 0, j))
        return arr3, pl.BlockSpec((1, tm, width), lambda i, j: (i, 0, 0))

    args = [x]
    specs = [pl.BlockSpec((tm, k), lambda i, j: (i, 0))]
    if mod is not None:
        gain, shift, scale = mod
        args.append(gain.reshape(1, k).astype(F32))
        specs.append(pl.BlockSpec((1, k), lambda i, j: (0, 0)))
        for a in (shift, scale):
            a3, sp = row_or_group(a, k, False)
            args.append(a3)
            specs.append(sp)
    if not only_mod:
        args.append(w)
        specs.append(pl.BlockSpec((k, tn), lambda i, j: (0, j)))
    if bias is not None:
        args.append(bias.reshape(1, n).astype(F32))
        specs.append(pl.BlockSpec((1, tn), lambda i, j: (0, j)))
    if epi is not None:
        res, gate = epi
        args.append(res)
        specs.append(pl.BlockSpec((tm, tn), lambda i, j: (i, j)))
        g3, sp = row_or_group(gate, n, True)
        args.append(g3)
        specs.append(sp)

    body = functools.partial(_linear_body, mod=mod is not None, pre_silu=pre_silu,
                             has_bias=bias is not None, epi=epi is not None, post=post,
                             n_j=n_j, only_mod=only_mod)
    scratch = [pltpu.VMEM((tm, k), BF16)] if n_j > 1 else []
    return pl.pallas_call(
        body,
        grid=(n_i, n_j),
        in_specs=specs,
        out_specs=pl.BlockSpec((tm, tn), lambda i, j: (i, j)),
        out_shape=jax.ShapeDtypeStruct((m, n), out_dtype),
        scratch_shapes=scratch,
        compiler_params=_cparams(("parallel", "arbitrary")),
        name=name,
    )(*args)


def _router_body(x_ref, gain_ref, shift_ref, scale_ref, wr_ref, br_ref, h_ref, ti_ref, tw_ref, cnt_ref):
    h = _modulated(x_ref[...], gain_ref[...], shift_ref[0], scale_ref[0]).astype(BF16)
    h_ref[...] = h
    logits = _dot(h, wr_ref[...]) + br_ref[...]
    lane = lax.broadcasted_iota(jnp.int32, logits.shape, 1).astype(F32)
    vals, idxs = [], []
    cur = logits
    for _ in range(TOP_K):
        mx = jnp.max(cur, axis=-1, keepdims=True)
        idx = jnp.min(jnp.where(cur == mx, lane, float(LANES)), axis=-1, keepdims=True)
        vals.append(mx)
        idxs.append(idx)
        cur = jnp.where(lane == idx, -jnp.inf, cur)
    exps = [jnp.exp(v - vals[0]) for v in vals]
    denom = exps[0] + exps[1] + exps[2] + exps[3]
    ti = jnp.zeros(logits.shape, F32)
    tw = jnp.zeros(logits.shape, F32)
    tm = logits.shape[0]
    earlier = (lax.broadcasted_iota(jnp.int32, (tm, tm), 1) < lax.broadcasted_iota(jnp.int32, (tm, tm), 0))
    earlier = jnp.where(earlier, 1.0, 0.0).astype(BF16)
    seen = jnp.zeros((1, LANES), F32)
    for kk in range(TOP_K):
        ti = jnp.where(lane == float(kk), idxs[kk], ti)
        tw = jnp.where(lane == float(kk), exps[kk] / denom, tw)
        onehot = jnp.where(lane == idxs[kk], 1.0, 0.0)
        above = _dot(earlier, onehot.astype(BF16))
        local = jnp.sum((above + seen) * onehot, axis=-1, keepdims=True)
        ti = jnp.where(lane == float(TOP_K + kk), local, ti)
        seen = seen + jnp.sum(onehot, axis=0, keepdims=True)
    ti_ref[...] = ti.astype(jnp.int32)
    tw_ref[...] = tw
    cnt_ref[...] = jnp.broadcast_to(seen, cnt_ref.shape).astype(jnp.int32)


def _router(x, gain, shift, scale, wr_pad, br_pad, *, tm, total_rows, row_offset=0, rows_per_group=None,
            bufs=None):
    m, k = x.shape
    tm = min(tm, m)
    n_i = m // tm
    assert row_offset % tm == 0
    off = row_offset // tm
    if shift.ndim == 3:
        tpg = rows_per_group // tm
        mod_spec = pl.BlockSpec((1, 1, k), lambda i: (i // tpg, 0, 0))
        sh3, sc3 = shift, scale
    else:
        mod_spec = pl.BlockSpec((1, tm, k), lambda i: (i, 0, 0))
        sh3, sc3 = shift.reshape(n_i, tm, k), scale.reshape(n_i, tm, k)
    args = [x, gain.reshape(1, k).astype(F32), sh3, sc3, wr_pad, br_pad]
    in_specs = [pl.BlockSpec((tm, k), lambda i: (i, 0)),
                pl.BlockSpec((1, k), lambda i: (0, 0)),
                mod_spec, mod_spec,
                pl.BlockSpec((k, LANES), lambda i: (0, 0)),
                pl.BlockSpec((1, LANES), lambda i: (0, 0))]
    aliases = {}
    body = _router_body
    if bufs is not None:
        aliases = {len(args) + j: j for j in range(3)}
        args += list(bufs)
        in_specs += [pl.BlockSpec(memory_space=pl.ANY)] * 3

        def body(*refs):
            _router_body(*refs[:6], *refs[9:])

    h_all, ti_all, tw_all, cnt = pl.pallas_call(
        body,
        grid=(n_i,),
        in_specs=in_specs,
        out_specs=[pl.BlockSpec((tm, k), lambda i: (i + off, 0)),
                   pl.BlockSpec((tm, LANES), lambda i: (i + off, 0)),
                   pl.BlockSpec((tm, LANES), lambda i: (i + off, 0)),
                   pl.BlockSpec((8, LANES), lambda i: (i, 0))],
        out_shape=[jax.ShapeDtypeStruct((total_rows, k), BF16),
                   jax.ShapeDtypeStruct((total_rows, LANES), jnp.int32),
                   jax.ShapeDtypeStruct((total_rows, LANES), F32),
                   jax.ShapeDtypeStruct((n_i * 8, LANES), jnp.int32)],
        input_output_aliases=aliases,
        compiler_params=_cparams(("parallel",)),
        name="moe_router",
    )(*args)
    return (h_all, ti_all, tw_all), cnt[::8, :N_EXPERTS]


MOE_TILE = 512
ROUTER_TILE = 512
GU_GROUP = 2 * LANES


def _regroup_gu_bias(b):
    lead = b.shape[:-1]
    g = b.reshape(*lead, -1, LANES, 2)
    return jnp.swapaxes(g, -1, -2).reshape(*lead, -1)


def _gmm_body(te_ref, tv_ref, hs_ref, w1_ref, b1_ref, w2_ref, b2_ref, perm_ref, o_ref, w1_s, w2_s):
    i = pl.program_id(0)
    new_expert = (i == 0) | (te_ref[i] != te_ref[jnp.maximum(i - 1, 0)])

    @pl.when(new_expert)
    def _():
        for c in range(2 * D_FF // GU_GROUP):
            cols = slice(c * GU_GROUP, (c + 1) * GU_GROUP)
            w1_s[:, cols] = _dot(w1_ref[0, :, cols].astype(BF16), perm_ref[...]).astype(BF16)
        w2_s[...] = w2_ref[0].astype(BF16)

    @pl.when(tv_ref[i] > 0)
    def _():
        gu = _dot(hs_ref[...], w1_s[...]) + b1_ref[0]
        acts = []
        for c in range(2 * D_FF // GU_GROUP):
            glu = jnp.minimum(gu[:, c * GU_GROUP:c * GU_GROUP + LANES], SWIGLU_LIMIT)
            lin = jnp.clip(gu[:, c * GU_GROUP + LANES:(c + 1) * GU_GROUP], -SWIGLU_LIMIT, SWIGLU_LIMIT)
            acts.append((glu * jax.nn.sigmoid(SWIGLU_ALPHA * glu) * (lin + 1.0)).astype(BF16))
        act = jnp.concatenate(acts, axis=1)
        y = _dot(act, w2_s[...]) + b2_ref[0]
        o_ref[...] = y.astype(o_ref.dtype)

    @pl.when(tv_ref[i] == 0)
    def _():
        o_ref[...] = jnp.zeros_like(o_ref)


def _gu_permutation():
    src = jnp.arange(GU_GROUP)
    dst = (src % 2) * LANES + src // 2
    return jnp.zeros((GU_GROUP, GU_GROUP), BF16).at[src, dst].set(1.0)


def _moe_gmm(tile_expert, tile_valid, tile_block, hs, w1, b1, w2, b2):
    p_rows, d = hs.shape
    n_tiles = p_rows // MOE_TILE
    grid_spec = pltpu.PrefetchScalarGridSpec(
        num_scalar_prefetch=3,
        grid=(n_tiles,),
        in_specs=[
            pl.BlockSpec((MOE_TILE, d), lambda i, te, tv, tb: (tb[i], 0)),
            pl.BlockSpec((1, d, 2 * D_FF), lambda i, te, tv, tb: (te[i], 0, 0)),
            pl.BlockSpec((1, 1, 2 * D_FF), lambda i, te, tv, tb: (te[i], 0, 0)),
            pl.BlockSpec((1, D_FF, d), lambda i, te, tv, tb: (te[i], 0, 0)),
            pl.BlockSpec((1, 1, d), lambda i, te, tv, tb: (te[i], 0, 0)),
            pl.BlockSpec((GU_GROUP, GU_GROUP), lambda i, te, tv, tb: (0, 0)),
        ],
        out_specs=pl.BlockSpec((MOE_TILE, d), lambda i, te, tv, tb: (i, 0)),
        scratch_shapes=[pltpu.VMEM((d, 2 * D_FF), BF16), pltpu.VMEM((D_FF, d), BF16)],
    )

    def body(te_ref, tv_ref, tb_ref, *rest):
        _gmm_body(te_ref, tv_ref, *rest)

    return pl.pallas_call(
        body,
        grid_spec=grid_spec,
        out_shape=jax.ShapeDtypeStruct((p_rows, d), BF16),
        compiler_params=_cparams(("arbitrary",)),
        name="moe_gmm",
    )(tile_expert, tile_valid, tile_block, hs, w1, b1, w2, b2, _gu_permutation())


def _combine_body(x_ref, y_ref, tw_ref, g_ref, o_ref):
    w = tw_ref[...]
    acc = y_ref[0].astype(F32) * w[:, 0:1]
    for kk in range(1, TOP_K):
        acc = acc + y_ref[kk].astype(F32) * w[:, kk:kk + 1]
    o_ref[...] = x_ref[...] + g_ref[0] * acc


def _moe_combine(x, yk, tw_all, gate, *, tm, row_offset, rows_per_group=None):
    m, d = x.shape
    tm = min(tm, m)
    n_i = m // tm
    off = row_offset // tm
    if gate.ndim == 3:
        tpg = rows_per_group // tm
        g3, g_spec = gate, pl.BlockSpec((1, 1, d), lambda i: (i // tpg, 0, 0))
    else:
        g3, g_spec = gate.reshape(n_i, tm, d), pl.BlockSpec((1, tm, d), lambda i: (i, 0, 0))
    return pl.pallas_call(
        _combine_body,
        grid=(n_i,),
        in_specs=[pl.BlockSpec((tm, d), lambda i: (i, 0)),
                  pl.BlockSpec((TOP_K, tm, d), lambda i: (0, i + off, 0)),
                  pl.BlockSpec((tm, LANES), lambda i: (i + off, 0)),
                  g_spec],
        out_specs=pl.BlockSpec((tm, d), lambda i: (i, 0)),
        out_shape=jax.ShapeDtypeStruct((m, d), F32),
        compiler_params=_cparams(("parallel",)),
        name="moe_combine",
    )(x, yk, tw_all, g3)


def _moe_sorted_rows(t):
    return ((t * TOP_K + N_EXPERTS * (MOE_TILE - 1)) // MOE_TILE + 1) * MOE_TILE


def _moe_layer(x_parts, h_all, ti_all, tw_all, tile_counts, tile_of_row, gates2, rows_per_group, expert_base,
               w1, b1, w2, b2):
    t = sum(xp.shape[0] for xp in x_parts)
    n_slots = t * TOP_K
    experts = ti_all[:t, :TOP_K]
    local = ti_all[:t, TOP_K:2 * TOP_K]
    counts = jnp.sum(tile_counts, axis=0)
    tile_base = (jnp.cumsum(tile_counts, axis=0) - tile_counts).reshape(-1)
    rank = jnp.take(tile_base, tile_of_row[:, None] * N_EXPERTS + experts) + local
    padded = ((counts + MOE_TILE - 1) // MOE_TILE) * MOE_TILE
    poffs = jnp.cumsum(padded) - padded
    total_padded = jnp.sum(padded)
    p_rows = _moe_sorted_rows(t)
    n_tiles = p_rows // MOE_TILE
    tile_start = jnp.arange(n_tiles, dtype=jnp.int32) * MOE_TILE
    tile_valid = (tile_start < total_padded).astype(jnp.int32)
    pends = poffs + padded
    tile_expert = jnp.minimum(jnp.sum((pends[None, :] <= tile_start[:, None]).astype(jnp.int32), axis=1),
                              N_EXPERTS - 1)
    last_valid = jnp.maximum(total_padded // MOE_TILE - 1, 0).astype(jnp.int32)
    tile_block = jnp.minimum(jnp.arange(n_tiles, dtype=jnp.int32), last_valid)
    tile_expert = jnp.where(tile_valid > 0, tile_expert, tile_expert[last_valid])
    inv = (jnp.take(poffs, experts) + rank).T.reshape(n_slots)
    row_token = (jnp.arange(p_rows, dtype=jnp.int32) % t).at[inv].set(jnp.arange(n_slots, dtype=jnp.int32) % t)

    hs = jnp.take(h_all, row_token, axis=0)
    ys = _moe_gmm(tile_expert + expert_base, tile_valid, tile_block, hs, w1, b1, w2, b2)
    yk = jnp.take(ys, inv, axis=0).reshape(TOP_K, t, -1)
    outs, start = [], 0
    for xp, g2, rpg in zip(x_parts, gates2, rows_per_group):
        r = xp.shape[0]
        outs.append(_moe_combine(xp, yk, tw_all, g2, tm=512, row_offset=start, rows_per_group=rpg))
        start += r
    return outs


def _rope_tables(pos):
    half = RET_QK_DIM // 2
    inv = 1.0 / (ROPE_BASE ** jnp.linspace(0.0, 1.0, half, dtype=F32))
    ang = jnp.repeat(pos.astype(F32)[:, None] * inv[None, :], 2, axis=-1)
    cos, sin = jnp.cos(ang), jnp.sin(ang)
    even = (jnp.arange(RET_QK_DIM) % 2 == 0)[None, :]
    return cos, jnp.where(even, -sin, 0.0), jnp.where(even, 0.0, sin)


def _rope(x, cos, sin_next, sin_prev):
    n = x.shape[-1]
    return x * cos + pltpu.roll(x, n - 1, 1) * sin_next + pltpu.roll(x, 1, 1) * sin_prev


def _log_gamma():
    return jnp.log1p(-jnp.exp2(-5.0 - jnp.arange(RET_HEADS, dtype=F32)))


RET_BLOCK = 256


def _ret_prompt_body(q_ref, k_ref, v_ref, g_ref, cos_ref, sn_ref, sp_ref, dec_ref, rd_ref,
                     o_ref, st_ref, state):
    c = pl.program_id(1)

    @pl.when(c == 0)
    def _():
        state[...] = jnp.zeros_like(state)

    cos, sn, sp = cos_ref[...], sn_ref[...], sp_ref[...]
    for h in range(RET_HEADS):
        qk_cols = slice(h * RET_QK_DIM, (h + 1) * RET_QK_DIM)
        v_cols = slice(h * RET_V_DIM, (h + 1) * RET_V_DIM)
        q = _rope(q_ref[0, :, qk_cols].astype(F32), cos, sn, sp)
        k = _rope(k_ref[0, :, qk_cols].astype(F32), cos, sn, sp) * (RET_QK_DIM ** -0.5)
        v = v_ref[0, :, v_cols]
        rd = rd_ref[h]
        q_dec = (q * rd[:, 0:1]).astype(BF16)
        k_dec = (k * rd[:, 1:2]).astype(BF16)
        inner = (_dot_nt(q.astype(BF16), k.astype(BF16)) * dec_ref[h]).astype(BF16)
        s_old = state[h]
        out = _dot(inner, v) + _dot(q_dec, s_old.astype(BF16))
        state[h] = rd[0:1, 2:3] * s_old + _dot_tn(k_dec, v)
        out = out * lax.rsqrt(jnp.mean(out * out, axis=-1, keepdims=True) + NORM_EPS)
        g = g_ref[0, :, v_cols].astype(F32)
        o_ref[0, :, v_cols] = (g * jax.nn.sigmoid(g) * out).astype(o_ref.dtype)

    @pl.when(c == pl.num_programs(1) - 1)
    def _():
        st_ref[0] = state[...]


def _retention_prompt(proj, batch, seq):
    c = RET_BLOCK
    n_chunks = seq // c
    proj3 = proj.reshape(batch, seq, -1)
    cos, sn, sp = _rope_tables(jnp.arange(seq, dtype=jnp.int32))
    lg = _log_gamma()
    idx = jnp.arange(c, dtype=F32)
    diff = idx[:, None] - idx[None, :]
    decay = jnp.where(diff >= 0, jnp.exp(jnp.maximum(diff, 0.0)[None] * lg[:, None, None]), 0.0)
    qd = jnp.exp((idx + 1.0)[None, :] * lg[:, None])
    kd = jnp.exp((c - 1.0 - idx)[None, :] * lg[:, None])
    cd = jnp.broadcast_to(jnp.exp(c * lg)[:, None], (RET_HEADS, c))
    rd = jnp.zeros((RET_HEADS, c, LANES), F32).at[:, :, 0].set(qd).at[:, :, 1].set(kd).at[:, :, 2].set(cd)
    out, st = pl.pallas_call(
        _ret_prompt_body,
        grid=(batch, n_chunks),
        in_specs=[
            pl.BlockSpec((1, c, RET_QK_WIDTH), lambda b, i: (b, i, 0)),
            pl.BlockSpec((1, c, RET_QK_WIDTH), lambda b, i: (b, i, 1)),
            pl.BlockSpec((1, c, RET_V_WIDTH), lambda b, i: (b, i, 1)),
            pl.BlockSpec((1, c, RET_V_WIDTH), lambda b, i: (b, i, 2)),
            pl.BlockSpec((c, RET_QK_DIM), lambda b, i: (i, 0)),
            pl.BlockSpec((c, RET_QK_DIM), lambda b, i: (i, 0)),
            pl.BlockSpec((c, RET_QK_DIM), lambda b, i: (i, 0)),
            pl.BlockSpec((RET_HEADS, c, c), lambda b, i: (0, 0, 0)),
            pl.BlockSpec((RET_HEADS, c, LANES), lambda b, i: (0, 0, 0)),
        ],
        out_specs=[
            pl.BlockSpec((1, c, RET_V_WIDTH), lambda b, i: (b, i, 0)),
            pl.BlockSpec((1, RET_HEADS, RET_QK_DIM, RET_V_DIM), lambda b, i: (b, 0, 0, 0)),
        ],
        out_shape=[jax.ShapeDtypeStruct((batch, seq, RET_V_WIDTH), BF16),
                   jax.ShapeDtypeStruct((batch, RET_HEADS, RET_QK_DIM, RET_V_DIM), F32)],
        scratch_shapes=[pltpu.VMEM((RET_HEADS, RET_QK_DIM, RET_V_DIM), F32)],
        compiler_params=_cparams(("parallel", "arbitrary")),
        name="retention_prompt",
    )(proj3, proj3, proj3, proj3, cos, sn, sp, decay, rd)
    return out.reshape(batch * seq, RET_V_WIDTH), st


def _ret_sample_body(p_ref, st_ref, rope_ref, gam_ref, o_ref, ns_ref):
    cos, sn, sp = rope_ref[0:1, :], rope_ref[1:2, :], rope_ref[2:3, :]
    row0 = lax.broadcasted_iota(jnp.int32, (8, RET_QK_DIM), 0) == 0
    for h in range(RET_HEADS):
        gamma = gam_ref[h:h + 1, 0:1]
        q = p_ref[0, :, h * RET_QK_DIM:(h + 1) * RET_QK_DIM].astype(F32)
        k = p_ref[0, :, RET_QK_WIDTH + h * RET_QK_DIM:RET_QK_WIDTH + (h + 1) * RET_QK_DIM].astype(F32)
        v0 = 2 * RET_QK_WIDTH + h * RET_V_DIM
        v = p_ref[0, :, v0:v0 + RET_V_DIM].astype(F32)
        g0 = 2 * RET_QK_WIDTH + RET_V_WIDTH + h * RET_V_DIM
        g = p_ref[0, :, g0:g0 + RET_V_DIM].astype(F32)
        q = _rope(jnp.broadcast_to(q, (8, RET_QK_DIM)), cos, sn, sp)
        k = _rope(jnp.broadcast_to(k, (8, RET_QK_DIM)), cos, sn, sp) * (RET_QK_DIM ** -0.5)
        qb = q.astype(BF16)
        kb = k.astype(BF16)
        vb = v.astype(BF16)
        s_old = st_ref[0, 0, h]
        qk = jnp.sum(qb.astype(F32) * kb.astype(F32), axis=-1, keepdims=True)
        out = (qk.astype(BF16).astype(F32) * vb.astype(F32)
               + _dot((q * gamma).astype(BF16), s_old.astype(BF16)))
        k_row0 = jnp.where(row0, kb, jnp.zeros_like(kb))
        v8 = jnp.broadcast_to(vb, (8, RET_V_DIM))
        ns_ref[0, 0, h] = gamma * s_old + _dot_tn(k_row0, v8)
        out = out[0:1]
        out = out * lax.rsqrt(jnp.mean(out * out, axis=-1, keepdims=True) + NORM_EPS)
        o_ref[0, :, h * RET_V_DIM:(h + 1) * RET_V_DIM] = (g * jax.nn.sigmoid(g) * out).astype(o_ref.dtype)


def _retention_sample(proj, states, layer, pos, new_states=None):
    b = proj.shape[0]
    cos, sn, sp = _rope_tables(jnp.full((1,), pos, jnp.int32))
    rope = jnp.zeros((8, RET_QK_DIM), F32).at[0].set(cos[0]).at[1].set(sn[0]).at[2].set(sp[0])
    gam = jnp.broadcast_to(jnp.exp(_log_gamma())[:, None], (RET_HEADS, LANES))
    width = proj.shape[1]
    state_spec = pl.BlockSpec((1, 1, RET_HEADS, RET_QK_DIM, RET_V_DIM), lambda i: (layer, i, 0, 0, 0))
    args = [proj.reshape(b, 1, width), states, rope, gam]
    in_specs = [pl.BlockSpec((1, 1, width), lambda i: (i, 0, 0)),
                state_spec,
                pl.BlockSpec((8, RET_QK_DIM), lambda i: (0, 0)),
                pl.BlockSpec((RET_HEADS, LANES), lambda i: (0, 0))]
    aliases = {}
    body = _ret_sample_body
    if new_states is not None:
        aliases = {len(args): 1}
        args.append(new_states)
        in_specs.append(pl.BlockSpec(memory_space=pl.ANY))

        def body(p_ref, st_ref, rope_ref, gam_ref, _prev, o_ref, ns_ref):
            _ret_sample_body(p_ref, st_ref, rope_ref, gam_ref, o_ref, ns_ref)

    out, ns = pl.pallas_call(
        body,
        grid=(b,),
        in_specs=in_specs,
        out_specs=[pl.BlockSpec((1, 1, RET_V_WIDTH), lambda i: (i, 0, 0)), state_spec],
        out_shape=[jax.ShapeDtypeStruct((b, 1, RET_V_WIDTH), BF16),
                   jax.ShapeDtypeStruct(states.shape, states.dtype)],
        input_output_aliases=aliases,
        compiler_params=_cparams(("parallel",)),
        name="retention_sample",
    )(*args)
    return out.reshape(b, RET_V_WIDTH), ns


FOX_TQ = 1024


def _fox_prompt_body(qi_ref, ki_ref, q_ref, k_ref, v_ref, fk_ref, o_ref, qa_s, m_s, acc_s):
    p = pl.program_id(2)
    qi, ki = qi_ref[p], ki_ref[p]
    tq = FOX_TQ
    lane = lax.broadcasted_iota(jnp.int32, (tq, LANES), 1)

    def head_lanes(hh):
        return (lane >= hh * FOX_HEAD_DIM) & (lane < (hh + 1) * FOX_HEAD_DIM)

    @pl.when(ki == 0)
    def _():
        q2 = q_ref[0].astype(F32) * (FOX_HEAD_DIM ** -0.5)
        for hh in range(2):
            qa_s[hh] = jnp.where(head_lanes(hh), q2, 0.0).astype(BF16)
        m_s[...] = jnp.full_like(m_s, MASK_VALUE)
        acc_s[...] = jnp.zeros_like(acc_s)

    def step(masked):
        k2 = k_ref[0]
        v2 = v_ref[0].astype(F32)
        if masked:
            row = lax.broadcasted_iota(jnp.int32, (tq, tq), 0)
            col = lax.broadcasted_iota(jnp.int32, (tq, tq), 1)
            keep = col <= row
        for hh in range(2):
            v_aug = jnp.where(head_lanes(hh), v2, 1.0).astype(BF16)
            s = _dot_nt(qa_s[hh], k2) - fk_ref[0, hh]
            if masked:
                s = jnp.where(keep, s, MASK_VALUE)
            m_prev = m_s[hh]
            m_next = jnp.maximum(m_prev, jnp.max(s, axis=1, keepdims=True))
            alpha = jnp.exp(m_prev - m_next)
            pr = jnp.exp(s - jnp.tile(m_next, (1, tq // LANES)))
            acc_s[hh] = alpha * acc_s[hh] + _dot(pr.astype(BF16), v_aug)
            m_s[hh] = m_next

    @pl.when(ki < qi)
    def _():
        step(False)

    @pl.when(ki == qi)
    def _():
        step(True)
        a0, a1 = acc_s[0], acc_s[1]
        o0 = a0 / pltpu.roll(a0, FOX_HEAD_DIM, 1)
        o1 = a1 / pltpu.roll(a1, FOX_HEAD_DIM, 1)
        o_ref[0] = jnp.where(lane < FOX_HEAD_DIM, o0, o1).astype(o_ref.dtype)


def _fox_prompt(q, k, v, fcum_t, batch, seq):
    tq = FOX_TQ
    n_q = seq // tq
    pairs = [(i, j) for i in range(n_q) for j in range(i + 1)]
    qi_tab = jnp.asarray([p[0] for p in pairs], jnp.int32)
    ki_tab = jnp.asarray([p[1] for p in pairs], jnp.int32)
    n_hp = FOX_HEADS // 2
    grid_spec = pltpu.PrefetchScalarGridSpec(
        num_scalar_prefetch=2,
        grid=(batch, n_hp, len(pairs)),
        in_specs=[
            pl.BlockSpec((1, tq, LANES), lambda b, h, p, qi, ki: (b, qi[p], h)),
            pl.BlockSpec((1, tq, LANES), lambda b, h, p, qi, ki: (b, ki[p], h)),
            pl.BlockSpec((1, tq, LANES), lambda b, h, p, qi, ki: (b, ki[p], h)),
            pl.BlockSpec((1, 2, 1, tq), lambda b, h, p, qi, ki: (b, h, 0, ki[p])),
        ],
        out_specs=pl.BlockSpec((1, tq, LANES), lambda b, h, p, qi, ki: (b, qi[p], h)),
        scratch_shapes=[pltpu.VMEM((2, tq, LANES), BF16), pltpu.VMEM((2, tq, LANES), F32),
                        pltpu.VMEM((2, tq, LANES), F32)],
    )
    return pl.pallas_call(
        _fox_prompt_body,
        grid_spec=grid_spec,
        out_shape=jax.ShapeDtypeStruct((batch, seq, FOX_HEADS * FOX_HEAD_DIM), BF16),
        compiler_params=_cparams(("parallel", "parallel", "arbitrary")),
        name="fox_prompt",
    )(qi_tab, ki_tab, q, k, v, fcum_t)


DEC_PAGES = 8


def _fox_decode_body(pt_ref, q_ref, kn_ref, vn_ref, bias_ref, *rest):
    k_refs, v_refs = rest[:DEC_PAGES], rest[DEC_PAGES:2 * DEC_PAGES]
    o_ref, m_s, l_s, acc_s = rest[2 * DEC_PAGES:]
    g = pl.program_id(1)
    width = FOX_HEADS * FOX_HEAD_DIM
    page = k_refs[0].shape[-1]
    lane = lax.broadcasted_iota(jnp.int32, (FOX_HEADS, width), 1)
    row = lax.broadcasted_iota(jnp.int32, (FOX_HEADS, width), 0)
    own = (lane >= row * FOX_HEAD_DIM) & (lane < (row + 1) * FOX_HEAD_DIM)
    scale = FOX_HEAD_DIM ** -0.5
    q_row = jnp.broadcast_to(q_ref[0].astype(F32), (FOX_HEADS, width))
    q_blk = jnp.where(own, q_row, 0.0).astype(BF16)

    @pl.when(g == 0)
    def _():
        m_s[...] = jnp.full_like(m_s, MASK_VALUE)
        l_s[...] = jnp.zeros_like(l_s)
        acc_s[...] = jnp.zeros_like(acc_s)

    k_t = jnp.concatenate([r[0].reshape(width, page).astype(BF16) for r in k_refs], axis=1)
    s = _dot(q_blk, k_t) * scale + bias_ref[0]
    m_prev = m_s[...]
    m_next = jnp.maximum(m_prev, jnp.max(s, axis=-1, keepdims=True))
    alpha = jnp.exp(m_prev - m_next)
    pr = jnp.exp(s - m_next)
    l_s[...] = alpha * l_s[...] + jnp.sum(pr, axis=-1, keepdims=True)
    prb = pr.astype(BF16)
    pv = _dot_nt(prb[:, 0:page], v_refs[0][0].reshape(width, page).astype(BF16))
    for i in range(1, DEC_PAGES):
        pv = pv + _dot_nt(prb[:, i * page:(i + 1) * page], v_refs[i][0].reshape(width, page).astype(BF16))
    acc_s[...] = alpha * acc_s[...] + pv
    m_s[...] = m_next

    @pl.when(g == pl.num_programs(1) - 1)
    def _():
        knb = kn_ref[0].astype(BF16).astype(F32)
        s_new = jnp.sum(q_blk.astype(F32) * knb, axis=-1, keepdims=True) * scale
        m_fin = jnp.maximum(m_next, s_new)
        a = jnp.exp(m_next - m_fin)
        p_new = jnp.exp(s_new - m_fin)
        denom = a * l_s[...] + p_new
        vnb = vn_ref[0].astype(BF16).astype(F32)
        o_full = (a * acc_s[...] + p_new.astype(BF16).astype(F32) * vnb) / denom
        o_ref[0] = jnp.sum(jnp.where(own, o_full, 0.0), axis=0, keepdims=True).astype(o_ref.dtype)


def _fox_decode(q, k_new, v_new, cache_kt, cache_vt, page_table, bias_t):
    b, width = q.shape
    n_pages = page_table.shape[1]
    _, heads, hd, page = cache_kt.shape
    n_groups = n_pages // DEC_PAGES
    vec = pl.BlockSpec((1, 1, width), lambda i, g, pt: (i, 0, 0))

    def page_spec(j):
        return pl.BlockSpec((1, heads, hd, page), lambda i, g, pt: (pt[i, g * DEC_PAGES + j], 0, 0, 0))

    grid_spec = pltpu.PrefetchScalarGridSpec(
        num_scalar_prefetch=1,
        grid=(b, n_groups),
        in_specs=[vec, vec, vec,
                  pl.BlockSpec((1, FOX_HEADS, DEC_PAGES * page), lambda i, g, pt: (i, 0, g))]
                 + [page_spec(j) for j in range(DEC_PAGES)] * 2,
        out_specs=vec,
        scratch_shapes=[pltpu.VMEM((FOX_HEADS, 1), F32), pltpu.VMEM((FOX_HEADS, 1), F32),
                        pltpu.VMEM((FOX_HEADS, width), F32)],
    )
    out = pl.pallas_call(
        _fox_decode_body,
        grid_spec=grid_spec,
        out_shape=jax.ShapeDtypeStruct((b, 1, width), BF16),
        compiler_params=_cparams(("parallel", "arbitrary")),
        name="fox_decode",
    )(page_table, q.reshape(b, 1, width), k_new.reshape(b, 1, width), v_new.reshape(b, 1, width), bias_t,
      *([cache_kt] * DEC_PAGES), *([cache_vt] * DEC_PAGES))
    return out.reshape(b, width)


def kernel(x_prompt, x_sample, state_ret, cache_k, cache_v, cache_logf, page_table, c_prompt, c_sample,
           norm_mix, norm_ffn, ada_w, ada_b, ret_w_in, ret_w_out, fox_w_q, fox_w_o, norm_kv, ada_kv_w,
           ada_kv_b, kv_w, f_w, f_b, router_w, router_b, expert_w_gu, expert_b_gu, expert_w_down,
           expert_b_down, norm_final, ada_final_w, ada_final_b):
    d = D_MODEL
    bp, seq, _ = x_prompt.shape
    bs = x_sample.shape[0]
    tp = bp * seq
    past_len = page_table.shape[1] * cache_k.shape[1]

    ret_w_in_b = ret_w_in.astype(BF16)
    ret_w_out_b = ret_w_out.astype(BF16)
    fox_w_q_b = fox_w_q.astype(BF16)
    fox_w_o_b = fox_w_o.astype(BF16)
    kv_w_b = kv_w.astype(BF16)
    f_w_pad = jnp.zeros((d, LANES), BF16).at[:, :FOX_HEADS].set(f_w.astype(BF16))
    f_b_pad = jnp.zeros((LANES,), F32).at[:FOX_HEADS].set(f_b)
    router_w_pad = jnp.zeros((DEPTH, d, LANES), BF16).at[:, :, :N_EXPERTS].set(router_w.astype(BF16))
    router_b_pad = jnp.full((DEPTH, 1, LANES), -jnp.inf, F32).at[:, 0, :N_EXPERTS].set(router_b)
    w1 = expert_w_gu.reshape(DEPTH * N_EXPERTS, d, 2 * D_FF)
    b1 = _regroup_gu_bias(expert_b_gu).reshape(DEPTH * N_EXPERTS, 1, 2 * D_FF)
    w2 = expert_w_down.reshape(DEPTH * N_EXPERTS, D_FF, d)
    b2 = expert_b_down.reshape(DEPTH * N_EXPERTS, 1, d)

    ada_all_w = jnp.concatenate([ada_w[l] for l in range(DEPTH)] + [ada_kv_w, ada_final_w], axis=1).astype(BF16)
    ada_all_b = jnp.concatenate([ada_b[l] for l in range(DEPTH)] + [ada_kv_b, ada_final_b], axis=0)
    n_c = bs + bp
    n_c_pad = ((n_c + 7) // 8) * 8
    c_all = jnp.zeros((n_c_pad, d), F32).at[:bs].set(c_sample).at[bs:n_c].set(c_prompt)
    mods = _linear(c_all, ada_all_w, pre_silu=True, bias=ada_all_b, tm=n_c_pad, tn=2048, name="ada_params")

    def mod_s(col):
        return mods[:bs, col * d:(col + 1) * d]

    def mod_p(col):
        return mods[bs:n_c, col * d:(col + 1) * d].reshape(bp, 1, d)

    xp = x_prompt.reshape(tp, d)
    xs = x_sample.reshape(bs, d)
    sample_pos = past_len
    cache_kt = jnp.transpose(cache_k, (0, 2, 3, 1))
    cache_vt = jnp.transpose(cache_v, (0, 2, 3, 1))

    tile_of_row = jnp.concatenate([jnp.arange(tp, dtype=jnp.int32) // ROUTER_TILE,
                                   jnp.full((bs,), tp // ROUTER_TILE, jnp.int32)])
    ret_states_p, ret_states_s = [], None
    kv_p = kv_s = None
    for layer in range(DEPTH):
        base = layer * N_MOD
        gain_mix = norm_mix[layer]
        mix_mod_p = (gain_mix, mod_p(base + 0), mod_p(base + 1))
        mix_mod_s = (gain_mix, mod_s(base + 0), mod_s(base + 1))
        if layer < N_A_LAYERS:
            proj_p = _linear(xp, ret_w_in_b[layer], mod=mix_mod_p, out_dtype=BF16, tn=2048,
                             rows_per_group=seq, name="ret_in_prompt")
            gated_p, st_p = _retention_prompt(proj_p, bp, seq)
            xp = _linear(gated_p, ret_w_out_b[layer], epi=(xp, mod_p(base + 2)), rows_per_group=seq,
                         name="ret_out_prompt")
            ret_states_p.append(st_p)
            proj_s = _linear(xs, ret_w_in_b[layer], mod=mix_mod_s, out_dtype=F32, tn=2048, name="ret_in_sample")
            gated_s, ret_states_s = _retention_sample(proj_s, state_ret, layer, sample_pos, ret_states_s)
            xs = _linear(gated_s, ret_w_out_b[layer], epi=(xs, mod_s(base + 2)), name="ret_out_sample")
        else:
            j = layer - N_A_LAYERS
            k_p, v_p, fcum_t, k_pb, v_pb = kv_p
            q_p = _linear(xp, fox_w_q_b[j], mod=mix_mod_p, out_dtype=BF16, rows_per_group=seq, name="fox_q_prompt")
            att_p = _fox_prompt(q_p.reshape(bp, seq, d), k_pb, v_pb, fcum_t, bp, seq)
            xp = _linear(att_p.reshape(tp, d), fox_w_o_b[j], epi=(xp, mod_p(base + 2)), rows_per_group=seq,
                         name="fox_o_prompt")
            k_s, v_s, bias_t = kv_s
            q_s = _linear(xs, fox_w_q_b[j], mod=mix_mod_s, out_dtype=F32, name="fox_q_sample")
            att_s = _fox_decode(q_s, k_s, v_s, cache_kt, cache_vt, page_table, bias_t)
            xs = _linear(att_s, fox_w_o_b[j], epi=(xs, mod_s(base + 2)), name="fox_o_sample")

        gain_ffn = norm_ffn[layer]
        buf_rows = _moe_sorted_rows(tp + bs)
        bufs, cnt_p = _router(xp, gain_ffn, mod_p(base + 3), mod_p(base + 4), router_w_pad[layer],
                              router_b_pad[layer], tm=ROUTER_TILE, total_rows=buf_rows, rows_per_group=seq)
        (h_all, ti_all, tw_all), cnt_s = _router(xs, gain_ffn, mod_s(base + 3), mod_s(base + 4), router_w_pad[layer],
                                                 router_b_pad[layer], tm=bs, total_rows=buf_rows, row_offset=tp,
                                                 bufs=bufs)
        xp, xs = _moe_layer([xp, xs], h_all, ti_all, tw_all, jnp.concatenate([cnt_p, cnt_s], axis=0), tile_of_row,
                            [mod_p(base + 5), mod_s(base + 5)], [seq, None], layer * N_EXPERTS, w1, b1, w2, b2)

        if layer == N_A_LAYERS - 1:
            kvb = DEPTH * N_MOD
            kv_mod_p = (norm_kv, mod_p(kvb + 0), mod_p(kvb + 1))
            kv_mod_s = (norm_kv, mod_s(kvb + 0), mod_s(kvb + 1))
            kvp = _linear(xp, kv_w_b, mod=kv_mod_p, rows_per_group=seq, name="kv_prompt")
            lf_p = _linear(xp, f_w_pad, mod=kv_mod_p, bias=f_b_pad, post="log_sigmoid", rows_per_group=seq,
                           name="logf_prompt")[:, :FOX_HEADS]
            k_p, v_p = kvp[:, :d], kvp[:, d:]
            logf_p = lf_p.reshape(bp, seq, FOX_HEADS)
            fcum_t = jnp.transpose(jnp.cumsum(logf_p, axis=1), (0, 2, 1))[:, :, None, :]
            kv_p = (k_p, v_p, fcum_t, k_p.astype(BF16).reshape(bp, seq, d), v_p.astype(BF16).reshape(bp, seq, d))

            kvs = _linear(xs, kv_w_b, mod=kv_mod_s, name="kv_sample")
            lf_s = _linear(xs, f_w_pad, mod=kv_mod_s, bias=f_b_pad, post="log_sigmoid",
                           name="logf_sample")[:, :FOX_HEADS]
            k_s, v_s = kvs[:, :d], kvs[:, d:]
            lf_past = cache_logf[page_table].reshape(bs, past_len, FOX_HEADS).astype(F32)
            cs = jnp.cumsum(lf_past, axis=1)
            bias = (cs[:, -1:, :] - cs) + lf_s[:, None, :]
            kv_s = (k_s, v_s, jnp.transpose(bias, (0, 2, 1)))

    fb = DEPTH * N_MOD + 2
    y_p = _linear(xp, None, mod=(norm_final, mod_p(fb + 0), mod_p(fb + 1)), rows_per_group=seq, name="final_prompt")
    y_s = _linear(xs, None, mod=(norm_final, mod_s(fb + 0), mod_s(fb + 1)), name="final_sample")

    k_p, v_p = kv_p[0], kv_p[1]
    k_s, v_s = kv_s[0], kv_s[1]
    return (y_p.reshape(bp, seq, d),
            y_s.reshape(bs, 1, d),
            jnp.stack(ret_states_p),
            ret_states_s,
            k_p.reshape(bp, seq, FOX_HEADS, FOX_HEAD_DIM),
            v_p.reshape(bp, seq, FOX_HEADS, FOX_HEAD_DIM),
            logf_p,
            k_s.reshape(bs, 1, FOX_HEADS, FOX_HEAD_DIM),
            v_s.reshape(bs, 1, FOX_HEADS, FOX_HEAD_DIM),
            lf_s.reshape(bs, 1, FOX_HEADS))
```

```python
import functools

import jax
import jax.numpy as jnp
import numpy as np
from jax import lax
from jax.experimental import pallas as pl
from jax.experimental.pallas import tpu as pltpu

F32 = jnp.float32
BF16 = jnp.bfloat16

D_MODEL = 1024
DEPTH = 4
N_A_LAYERS = 2
RET_HEADS = 4
RET_QK_DIM = 256
RET_V_DIM = 512
RET_QK_WIDTH = 1024
RET_V_WIDTH = 2048
RET_CHUNK = 128
ROPE_BASE = 10000.0
FOX_HEADS = 16
FOX_HEAD_DIM = 64
N_EXPERTS = 32
TOP_K = 4
D_FF = 1024
SWIGLU_LIMIT = 7.0
SWIGLU_ALPHA = 1.702
NORM_EPS = 1e-6
N_MOD = 6
MASK_VALUE = -1e30

V7X_VMEM_LIMIT = 56 * 1024 * 1024
LANES = 128


def _cparams(sem):
    return pltpu.CompilerParams(dimension_semantics=sem, vmem_limit_bytes=V7X_VMEM_LIMIT)


def _dot(a, b):
    return jnp.dot(a, b, preferred_element_type=F32)


def _dot_nt(a, b):
    return lax.dot_general(a, b, (((1,), (1,)), ((), ())), preferred_element_type=F32)


def _dot_tn(a, b):
    return lax.dot_general(a, b, (((0,), (0,)), ((), ())), preferred_element_type=F32)


def _modulated(x, gain, shift, scale):
    xf = x.astype(F32)
    y = xf * lax.rsqrt(jnp.mean(xf * xf, axis=-1, keepdims=True) + NORM_EPS)
    return (y * gain) * (1.0 + scale) + shift


def _log_sigmoid(z):
    return jnp.minimum(z, 0.0) - jnp.log1p(jnp.exp(-jnp.abs(z)))


def _linear_body(*refs, mod, pre_silu, has_bias, epi, post, n_j, only_mod):
    it = iter(refs)
    x_ref = next(it)
    if mod:
        gain_ref, shift_ref, scale_ref = next(it), next(it), next(it)
    if not only_mod:
        w_ref = next(it)
    b_ref = next(it) if has_bias else None
    if epi:
        res_ref, gate_ref = next(it), next(it)
    o_ref = next(it)
    h_ref = next(it) if n_j > 1 else None

    def prologue():
        x = x_ref[...]
        if mod:
            return _modulated(x, gain_ref[...], shift_ref[0], scale_ref[0])
        if pre_silu:
            xf = x.astype(F32)
            return xf * jax.nn.sigmoid(xf)
        return x

    if only_mod:
        o_ref[...] = prologue().astype(o_ref.dtype)
        return

    if n_j > 1:
        @pl.when(pl.program_id(1) == 0)
        def _():
            h_ref[...] = prologue().astype(BF16)
        h = h_ref[...]
    else:
        h = prologue().astype(BF16)

    acc = _dot(h, w_ref[...])
    if has_bias:
        acc = acc + b_ref[...]
    if post == "log_sigmoid":
        acc = _log_sigmoid(acc)
    if epi:
        acc = res_ref[...] + gate_ref[0] * acc
    o_ref[...] = acc.astype(o_ref.dtype)


def _linear(x, w=None, *, mod=None, pre_silu=False, bias=None, epi=None, post=None,
            out_dtype=F32, tm=512, tn=1024, rows_per_group=None, name="linear"):
    m, k = x.shape
    only_mod = w is None
    n = k if only_mod else w.shape[1]
    tm = min(tm, m)
    tn = n if only_mod else min(tn, n)
    assert m % tm == 0 and n % tn == 0
    n_i, n_j = m // tm, n // tn
    if only_mod:
        n_j = 1

    def row_or_group(arr, width, col_tiled):
        if arr.ndim == 3:
            tiles_per_group = rows_per_group // tm
            assert rows_per_group % tm == 0
            if col_tiled:
                return arr, pl.BlockSpec((1, 1, tn), lambda i, j: (i // tiles_per_group, 0, j))
            return arr, pl.BlockSpec((1, 1, width), lambda i, j: (i // tiles_per_group, 0, 0))
        arr3 = arr.reshape(n_i, tm, width)
        if col_tiled:
            return arr3, pl.BlockSpec((1, tm, tn), lambda i, j: (i, 0, j))
        return arr3, pl.BlockSpec((1, tm, width), lambda i, j: (i, 0, 0))

    args = [x]
    specs = [pl.BlockSpec((tm, k), lambda i, j: (i, 0))]
    if mod is not None:
        gain, shift, scale = mod
        args.append(gain.reshape(1, k).astype(F32))
        specs.append(pl.BlockSpec((1, k), lambda i, j: (0, 0)))
        for a in (shift, scale):
            a3, sp = row_or_group(a, k, False)
            args.append(a3)
            specs.append(sp)
    if not only_mod:
        args.append(w)
        specs.append(pl.BlockSpec((k, tn), lambda i, j: (0, j)))
    if bias is not None:
        args.append(bias.reshape(1, n).astype(F32))
        specs.append(pl.BlockSpec((1, tn), lambda i, j: (0, j)))
    if epi is not None:
        res, gate = epi
        args.append(res)
        specs.append(pl.BlockSpec((tm, tn), lambda i, j: (i, j)))
        g3, sp = row_or_group(gate, n, True)
        args.append(g3)
        specs.append(sp)

    body = functools.partial(_linear_body, mod=mod is not None, pre_silu=pre_silu,
                             has_bias=bias is not None, epi=epi is not None, post=post,
                             n_j=n_j, only_mod=only_mod)
    scratch = [pltpu.VMEM((tm, k), BF16)] if n_j > 1 else []
    return pl.pallas_call(
        body,
        grid=(n_i, n_j),
        in_specs=specs,
        out_specs=pl.BlockSpec((tm, tn), lambda i, j: (i, j)),
        out_shape=jax.ShapeDtypeStruct((m, n), out_dtype),
        scratch_shapes=scratch,
        compiler_params=_cparams(("parallel", "arbitrary")),
        name=name,
    )(*args)


def _router_body(x_ref, gain_ref, shift_ref, scale_ref, wr_ref, br_ref, cnt0_ref, h_ref, ti_ref, tw_ref, cnt_ref,
                 run_s):
    @pl.when(pl.program_id(0) == 0)
    def _():
        run_s[...] = cnt0_ref[0:1, :].astype(F32)

    h = _modulated(x_ref[...], gain_ref[...], shift_ref[0], scale_ref[0]).astype(BF16)
    h_ref[...] = h
    logits = _dot(h, wr_ref[...]) + br_ref[...]
    lane = lax.broadcasted_iota(jnp.int32, logits.shape, 1).astype(F32)
    vals, idxs = [], []
    cur = logits
    for _ in range(TOP_K):
        mx = jnp.max(cur, axis=-1, keepdims=True)
        idx = jnp.min(jnp.where(cur == mx, lane, float(LANES)), axis=-1, keepdims=True)
        vals.append(mx)
        idxs.append(idx)
        cur = jnp.where(lane == idx, -jnp.inf, cur)
    exps = [jnp.exp(v - vals[0]) for v in vals]
    denom = exps[0] + exps[1] + exps[2] + exps[3]
    ti = jnp.zeros(logits.shape, F32)
    tw = jnp.zeros(logits.shape, F32)
    tm = logits.shape[0]
    earlier = (lax.broadcasted_iota(jnp.int32, (tm, tm), 1) < lax.broadcasted_iota(jnp.int32, (tm, tm), 0))
    earlier = jnp.where(earlier, 1.0, 0.0).astype(BF16)
    seen = run_s[...]
    for kk in range(TOP_K):
        ti = jnp.where(lane == float(kk), idxs[kk], ti)
        tw = jnp.where(lane == float(kk), exps[kk] / denom, tw)
        onehot = jnp.where(lane == idxs[kk], 1.0, 0.0)
        above = _dot(earlier, onehot.astype(BF16))
        local = jnp.sum((above + seen) * onehot, axis=-1, keepdims=True)
        ti = jnp.where(lane == float(TOP_K + kk), local, ti)
        seen = seen + jnp.sum(onehot, axis=0, keepdims=True)
    ti_ref[...] = ti.astype(jnp.int32)
    tw_ref[...] = tw
    run_s[...] = seen
    cnt_ref[...] = jnp.broadcast_to(seen, cnt_ref.shape).astype(jnp.int32)


def _router(x, gain, shift, scale, wr_pad, br_pad, *, tm, total_rows, row_offset=0, rows_per_group=None,
            bufs=None, counts=None):
    m, k = x.shape
    tm = min(tm, m)
    n_i = m // tm
    assert row_offset % tm == 0
    off = row_offset // tm
    if shift.ndim == 3:
        tpg = rows_per_group // tm
        mod_spec = pl.BlockSpec((1, 1, k), lambda i: (i // tpg, 0, 0))
        sh3, sc3 = shift, scale
    else:
        mod_spec = pl.BlockSpec((1, tm, k), lambda i: (i, 0, 0))
        sh3, sc3 = shift.reshape(n_i, tm, k), scale.reshape(n_i, tm, k)
    if counts is None:
        counts = jnp.zeros((8, LANES), jnp.int32)
    args = [x, gain.reshape(1, k).astype(F32), sh3, sc3, wr_pad, br_pad, counts]
    in_specs = [pl.BlockSpec((tm, k), lambda i: (i, 0)),
                pl.BlockSpec((1, k), lambda i: (0, 0)),
                mod_spec, mod_spec,
                pl.BlockSpec((k, LANES), lambda i: (0, 0)),
                pl.BlockSpec((1, LANES), lambda i: (0, 0)),
                pl.BlockSpec((8, LANES), lambda i: (0, 0))]
    n_in = len(args)
    aliases = {}
    body = _router_body
    if bufs is not None:
        aliases = {n_in + j: j for j in range(3)}
        args += list(bufs)
        in_specs += [pl.BlockSpec(memory_space=pl.ANY)] * 3

        def body(*refs):
            _router_body(*refs[:n_in], *refs[n_in + 3:])

    h_all, ti_all, tw_all, cnt = pl.pallas_call(
        body,
        grid=(n_i,),
        in_specs=in_specs,
        out_specs=[pl.BlockSpec((tm, k), lambda i: (i + off, 0)),
                   pl.BlockSpec((tm, LANES), lambda i: (i + off, 0)),
                   pl.BlockSpec((tm, LANES), lambda i: (i + off, 0)),
                   pl.BlockSpec((8, LANES), lambda i: (0, 0))],
        out_shape=[jax.ShapeDtypeStruct((total_rows, k), BF16),
                   jax.ShapeDtypeStruct((total_rows, LANES), jnp.int32),
                   jax.ShapeDtypeStruct((total_rows, LANES), F32),
                   jax.ShapeDtypeStruct((8, LANES), jnp.int32)],
        scratch_shapes=[pltpu.VMEM((1, LANES), F32)],
        input_output_aliases=aliases,
        compiler_params=_cparams(("arbitrary",)),
        name="moe_router",
    )(*args)
    return (h_all, ti_all, tw_all), cnt


MOE_TILE = 512
ROUTER_TILE = 512
GU_GROUP = 2 * LANES


def _regroup_gu_bias(b):
    lead = b.shape[:-1]
    g = b.reshape(*lead, -1, LANES, 2)
    return jnp.swapaxes(g, -1, -2).reshape(*lead, -1)


def _gmm_body(te_ref, tv_ref, hs_ref, w1_ref, b1_ref, w2_ref, b2_ref, perm_ref, o_ref, w1_s, w2_s):
    i = pl.program_id(0)
    new_expert = (i == 0) | (te_ref[i] != te_ref[jnp.maximum(i - 1, 0)])

    @pl.when(new_expert)
    def _():
        for c in range(2 * D_FF // GU_GROUP):
            cols = slice(c * GU_GROUP, (c + 1) * GU_GROUP)
            w1_s[:, cols] = _dot(w1_ref[0, :, cols].astype(BF16), perm_ref[...]).astype(BF16)
        w2_s[...] = w2_ref[0].astype(BF16)

    @pl.when(tv_ref[i] > 0)
    def _():
        gu = _dot(hs_ref[...], w1_s[...]) + b1_ref[0]
        acts = []
        for c in range(2 * D_FF // GU_GROUP):
            glu = jnp.minimum(gu[:, c * GU_GROUP:c * GU_GROUP + LANES], SWIGLU_LIMIT)
            lin = jnp.clip(gu[:, c * GU_GROUP + LANES:(c + 1) * GU_GROUP], -SWIGLU_LIMIT, SWIGLU_LIMIT)
            acts.append((glu * jax.nn.sigmoid(SWIGLU_ALPHA * glu) * (lin + 1.0)).astype(BF16))
        act = jnp.concatenate(acts, axis=1)
        y = _dot(act, w2_s[...]) + b2_ref[0]
        o_ref[...] = y.astype(o_ref.dtype)

    @pl.when(tv_ref[i] == 0)
    def _():
        o_ref[...] = jnp.zeros_like(o_ref)


def _gu_permutation():
    src = jnp.arange(GU_GROUP)
    dst = (src % 2) * LANES + src // 2
    return jnp.zeros((GU_GROUP, GU_GROUP), BF16).at[src, dst].set(1.0)


def _moe_gmm(tile_expert, tile_valid, tile_block, hs, w1, b1, w2, b2):
    p_rows, d = hs.shape
    n_tiles = p_rows // MOE_TILE
    grid_spec = pltpu.PrefetchScalarGridSpec(
        num_scalar_prefetch=3,
        grid=(n_tiles,),
        in_specs=[
            pl.BlockSpec((MOE_TILE, d), lambda i, te, tv, tb: (tb[i], 0)),
            pl.BlockSpec((1, d, 2 * D_FF), lambda i, te, tv, tb: (te[i], 0, 0)),
            pl.BlockSpec((1, 1, 2 * D_FF), lambda i, te, tv, tb: (te[i], 0, 0)),
            pl.BlockSpec((1, D_FF, d), lambda i, te, tv, tb: (te[i], 0, 0)),
            pl.BlockSpec((1, 1, d), lambda i, te, tv, tb: (te[i], 0, 0)),
            pl.BlockSpec((GU_GROUP, GU_GROUP), lambda i, te, tv, tb: (0, 0)),
        ],
        out_specs=pl.BlockSpec((MOE_TILE, d), lambda i, te, tv, tb: (i, 0)),
        scratch_shapes=[pltpu.VMEM((d, 2 * D_FF), BF16), pltpu.VMEM((D_FF, d), BF16)],
    )

    def body(te_ref, tv_ref, tb_ref, *rest):
        _gmm_body(te_ref, tv_ref, *rest)

    return pl.pallas_call(
        body,
        grid_spec=grid_spec,
        out_shape=jax.ShapeDtypeStruct((p_rows, d), BF16),
        compiler_params=_cparams(("arbitrary",)),
        name="moe_gmm",
    )(tile_expert, tile_valid, tile_block, hs, w1, b1, w2, b2, _gu_permutation())


def _combine_body(x_ref, y_ref, tw_ref, g_ref, o_ref):
    w = tw_ref[...]
    acc = y_ref[0].astype(F32) * w[:, 0:1]
    for kk in range(1, TOP_K):
        acc = acc + y_ref[kk].astype(F32) * w[:, kk:kk + 1]
    o_ref[...] = x_ref[...] + g_ref[0] * acc


def _moe_combine(x, yk, tw_all, gate, *, tm, row_offset, rows_per_group=None):
    m, d = x.shape
    tm = min(tm, m)
    n_i = m // tm
    off = row_offset // tm
    if gate.ndim == 3:
        tpg = rows_per_group // tm
        g3, g_spec = gate, pl.BlockSpec((1, 1, d), lambda i: (i // tpg, 0, 0))
    else:
        g3, g_spec = gate.reshape(n_i, tm, d), pl.BlockSpec((1, tm, d), lambda i: (i, 0, 0))
    return pl.pallas_call(
        _combine_body,
        grid=(n_i,),
        in_specs=[pl.BlockSpec((tm, d), lambda i: (i, 0)),
                  pl.BlockSpec((TOP_K, tm, d), lambda i: (0, i + off, 0)),
                  pl.BlockSpec((tm, LANES), lambda i: (i + off, 0)),
                  g_spec],
        out_specs=pl.BlockSpec((tm, d), lambda i: (i, 0)),
        out_shape=jax.ShapeDtypeStruct((m, d), F32),
        compiler_params=_cparams(("parallel",)),
        name="moe_combine",
    )(x, yk, tw_all, g3)


def _moe_sorted_rows(t):
    return ((t * TOP_K + N_EXPERTS * (MOE_TILE - 1)) // MOE_TILE + 1) * MOE_TILE


def _moe_layer(x_parts, h_all, ti_all, tw_all, counts8, gates2, rows_per_group, expert_base, w1, b1, w2, b2):
    t = sum(xp.shape[0] for xp in x_parts)
    n_slots = t * TOP_K
    experts = ti_all[:t, :TOP_K]
    rank = ti_all[:t, TOP_K:2 * TOP_K]
    counts = counts8[0, :N_EXPERTS]
    padded = ((counts + MOE_TILE - 1) // MOE_TILE) * MOE_TILE
    poffs = jnp.cumsum(padded) - padded
    total_padded = jnp.sum(padded)
    p_rows = _moe_sorted_rows(t)
    n_tiles = p_rows // MOE_TILE
    tile_start = jnp.arange(n_tiles, dtype=jnp.int32) * MOE_TILE
    tile_valid = (tile_start < total_padded).astype(jnp.int32)
    pends = poffs + padded
    tile_expert = jnp.minimum(jnp.sum((pends[None, :] <= tile_start[:, None]).astype(jnp.int32), axis=1),
                              N_EXPERTS - 1)
    last_valid = jnp.maximum(total_padded // MOE_TILE - 1, 0).astype(jnp.int32)
    tile_block = jnp.minimum(jnp.arange(n_tiles, dtype=jnp.int32), last_valid)
    tile_expert = jnp.where(tile_valid > 0, tile_expert, tile_expert[last_valid])
    is_expert = experts[:, :, None] == jnp.arange(N_EXPERTS, dtype=jnp.int32)[None, None, :]
    group_start = jnp.sum(jnp.where(is_expert, poffs[None, None, :], 0), axis=-1)
    inv = (group_start + rank).T.reshape(n_slots)
    row_token = (jnp.arange(p_rows, dtype=jnp.int32) % t).at[inv].set(jnp.arange(n_slots, dtype=jnp.int32) % t)

    hs = h_all.at[row_token].get(mode="promise_in_bounds")
    ys = _moe_gmm(tile_expert + expert_base, tile_valid, tile_block, hs, w1, b1, w2, b2)
    yk = ys.at[inv].get(mode="promise_in_bounds").reshape(TOP_K, t, -1)
    outs, start = [], 0
    for xp, g2, rpg in zip(x_parts, gates2, rows_per_group):
        r = xp.shape[0]
        outs.append(_moe_combine(xp, yk, tw_all, g2, tm=512, row_offset=start, rows_per_group=rpg))
        start += r
    return outs


def _rope_tables(pos):
    half = RET_QK_DIM // 2
    inv = 1.0 / (ROPE_BASE ** jnp.linspace(0.0, 1.0, half, dtype=F32))
    ang = jnp.repeat(pos.astype(F32)[:, None] * inv[None, :], 2, axis=-1)
    cos, sin = jnp.cos(ang), jnp.sin(ang)
    even = (jnp.arange(RET_QK_DIM) % 2 == 0)[None, :]
    return cos, jnp.where(even, -sin, 0.0), jnp.where(even, 0.0, sin)


def _rope(x, cos, sin_next, sin_prev):
    n = x.shape[-1]
    return x * cos + pltpu.roll(x, n - 1, 1) * sin_next + pltpu.roll(x, 1, 1) * sin_prev


def _log_gamma():
    return jnp.log1p(-jnp.exp2(-5.0 - jnp.arange(RET_HEADS, dtype=F32)))


RET_BLOCK = 256


def _ret_prompt_body(q_ref, k_ref, v_ref, g_ref, cos_ref, sn_ref, sp_ref, dec_ref, rd_ref,
                     o_ref, st_ref, state):
    c = pl.program_id(1)

    @pl.when(c == 0)
    def _():
        state[...] = jnp.zeros_like(state)

    cos, sn, sp = cos_ref[...], sn_ref[...], sp_ref[...]
    for h in range(RET_HEADS):
        qk_cols = slice(h * RET_QK_DIM, (h + 1) * RET_QK_DIM)
        v_cols = slice(h * RET_V_DIM, (h + 1) * RET_V_DIM)
        q = _rope(q_ref[0, :, qk_cols].astype(F32), cos, sn, sp)
        k = _rope(k_ref[0, :, qk_cols].astype(F32), cos, sn, sp) * (RET_QK_DIM ** -0.5)
        v = v_ref[0, :, v_cols]
        rd = rd_ref[h]
        q_dec = (q * rd[:, 0:1]).astype(BF16)
        k_dec = (k * rd[:, 1:2]).astype(BF16)
        inner = (_dot_nt(q.astype(BF16), k.astype(BF16)) * dec_ref[h]).astype(BF16)
        s_old = state[h]
        out = _dot(inner, v) + _dot(q_dec, s_old.astype(BF16))
        state[h] = rd[0:1, 2:3] * s_old + _dot_tn(k_dec, v)
        out = out * lax.rsqrt(jnp.mean(out * out, axis=-1, keepdims=True) + NORM_EPS)
        g = g_ref[0, :, v_cols].astype(F32)
        o_ref[0, :, v_cols] = (g * jax.nn.sigmoid(g) * out).astype(o_ref.dtype)

    @pl.when(c == pl.num_programs(1) - 1)
    def _():
        st_ref[0] = state[...]


def _retention_prompt(proj, batch, seq):
    c = RET_BLOCK
    n_chunks = seq // c
    proj3 = proj.reshape(batch, seq, -1)
    cos, sn, sp = _rope_tables(jnp.arange(seq, dtype=jnp.int32))
    lg = _log_gamma()
    idx = jnp.arange(c, dtype=F32)
    diff = idx[:, None] - idx[None, :]
    decay = jnp.where(diff >= 0, jnp.exp(jnp.maximum(diff, 0.0)[None] * lg[:, None, None]), 0.0)
    qd = jnp.exp((idx + 1.0)[None, :] * lg[:, None])
    kd = jnp.exp((c - 1.0 - idx)[None, :] * lg[:, None])
    cd = jnp.broadcast_to(jnp.exp(c * lg)[:, None], (RET_HEADS, c))
    rd = jnp.zeros((RET_HEADS, c, LANES), F32).at[:, :, 0].set(qd).at[:, :, 1].set(kd).at[:, :, 2].set(cd)
    out, st = pl.pallas_call(
        _ret_prompt_body,
        grid=(batch, n_chunks),
        in_specs=[
            pl.BlockSpec((1, c, RET_QK_WIDTH), lambda b, i: (b, i, 0)),
            pl.BlockSpec((1, c, RET_QK_WIDTH), lambda b, i: (b, i, 1)),
            pl.BlockSpec((1, c, RET_V_WIDTH), lambda b, i: (b, i, 1)),
            pl.BlockSpec((1, c, RET_V_WIDTH), lambda b, i: (b, i, 2)),
            pl.BlockSpec((c, RET_QK_DIM), lambda b, i: (i, 0)),
            pl.BlockSpec((c, RET_QK_DIM), lambda b, i: (i, 0)),
            pl.BlockSpec((c, RET_QK_DIM), lambda b, i: (i, 0)),
            pl.BlockSpec((RET_HEADS, c, c), lambda b, i: (0, 0, 0)),
            pl.BlockSpec((RET_HEADS, c, LANES), lambda b, i: (0, 0, 0)),
        ],
        out_specs=[
            pl.BlockSpec((1, c, RET_V_WIDTH), lambda b, i: (b, i, 0)),
            pl.BlockSpec((1, RET_HEADS, RET_QK_DIM, RET_V_DIM), lambda b, i: (b, 0, 0, 0)),
        ],
        out_shape=[jax.ShapeDtypeStruct((batch, seq, RET_V_WIDTH), BF16),
                   jax.ShapeDtypeStruct((batch, RET_HEADS, RET_QK_DIM, RET_V_DIM), F32)],
        scratch_shapes=[pltpu.VMEM((RET_HEADS, RET_QK_DIM, RET_V_DIM), F32)],
        compiler_params=_cparams(("parallel", "arbitrary")),
        name="retention_prompt",
    )(proj3, proj3, proj3, proj3, cos, sn, sp, decay, rd)
    return out.reshape(batch * seq, RET_V_WIDTH), st


def _ret_sample_body(p_ref, st_ref, rope_ref, gam_ref, o_ref, ns_ref):
    cos, sn, sp = rope_ref[0:1, :], rope_ref[1:2, :], rope_ref[2:3, :]
    row0 = lax.broadcasted_iota(jnp.int32, (8, RET_QK_DIM), 0) == 0
    for h in range(RET_HEADS):
        gamma = gam_ref[h:h + 1, 0:1]
        q = p_ref[0, :, h * RET_QK_DIM:(h + 1) * RET_QK_DIM].astype(F32)
        k = p_ref[0, :, RET_QK_WIDTH + h * RET_QK_DIM:RET_QK_WIDTH + (h + 1) * RET_QK_DIM].astype(F32)
        v0 = 2 * RET_QK_WIDTH + h * RET_V_DIM
        v = p_ref[0, :, v0:v0 + RET_V_DIM].astype(F32)
        g0 = 2 * RET_QK_WIDTH + RET_V_WIDTH + h * RET_V_DIM
        g = p_ref[0, :, g0:g0 + RET_V_DIM].astype(F32)
        q = _rope(jnp.broadcast_to(q, (8, RET_QK_DIM)), cos, sn, sp)
        k = _rope(jnp.broadcast_to(k, (8, RET_QK_DIM)), cos, sn, sp) * (RET_QK_DIM ** -0.5)
        qb = q.astype(BF16)
        kb = k.astype(BF16)
        vb = v.astype(BF16)
        s_old = st_ref[0, 0, h]
        qk = jnp.sum(qb.astype(F32) * kb.astype(F32), axis=-1, keepdims=True)
        out = (qk.astype(BF16).astype(F32) * vb.astype(F32)
               + _dot((q * gamma).astype(BF16), s_old.astype(BF16)))
        k_row0 = jnp.where(row0, kb, jnp.zeros_like(kb))
        v8 = jnp.broadcast_to(vb, (8, RET_V_DIM))
        ns_ref[0, 0, h] = gamma * s_old + _dot_tn(k_row0, v8)
        out = out[0:1]
        out = out * lax.rsqrt(jnp.mean(out * out, axis=-1, keepdims=True) + NORM_EPS)
        o_ref[0, :, h * RET_V_DIM:(h + 1) * RET_V_DIM] = (g * jax.nn.sigmoid(g) * out).astype(o_ref.dtype)


def _retention_sample(proj, states, layer, pos, new_states=None):
    b = proj.shape[0]
    cos, sn, sp = _rope_tables(jnp.full((1,), pos, jnp.int32))
    rope = jnp.zeros((8, RET_QK_DIM), F32).at[0].set(cos[0]).at[1].set(sn[0]).at[2].set(sp[0])
    gam = jnp.broadcast_to(jnp.exp(_log_gamma())[:, None], (RET_HEADS, LANES))
    width = proj.shape[1]
    state_spec = pl.BlockSpec((1, 1, RET_HEADS, RET_QK_DIM, RET_V_DIM), lambda i: (layer, i, 0, 0, 0))
    args = [proj.reshape(b, 1, width), states, rope, gam]
    in_specs = [pl.BlockSpec((1, 1, width), lambda i: (i, 0, 0)),
                state_spec,
                pl.BlockSpec((8, RET_QK_DIM), lambda i: (0, 0)),
                pl.BlockSpec((RET_HEADS, LANES), lambda i: (0, 0))]
    aliases = {}
    body = _ret_sample_body
    if new_states is not None:
        aliases = {len(args): 1}
        args.append(new_states)
        in_specs.append(pl.BlockSpec(memory_space=pl.ANY))

        def body(p_ref, st_ref, rope_ref, gam_ref, _prev, o_ref, ns_ref):
            _ret_sample_body(p_ref, st_ref, rope_ref, gam_ref, o_ref, ns_ref)

    out, ns = pl.pallas_call(
        body,
        grid=(b,),
        in_specs=in_specs,
        out_specs=[pl.BlockSpec((1, 1, RET_V_WIDTH), lambda i: (i, 0, 0)), state_spec],
        out_shape=[jax.ShapeDtypeStruct((b, 1, RET_V_WIDTH), BF16),
                   jax.ShapeDtypeStruct(states.shape, states.dtype)],
        input_output_aliases=aliases,
        compiler_params=_cparams(("parallel",)),
        name="retention_sample",
    )(*args)
    return out.reshape(b, RET_V_WIDTH), ns


FOX_TQ = 1024


def _fox_prompt_body(qi_ref, ki_ref, q_ref, k_ref, v_ref, fk_ref, o_ref, qa_s, m_s, acc_s):
    p = pl.program_id(2)
    qi, ki = qi_ref[p], ki_ref[p]
    tq = FOX_TQ
    lane = lax.broadcasted_iota(jnp.int32, (tq, LANES), 1)

    def head_lanes(hh):
        return (lane >= hh * FOX_HEAD_DIM) & (lane < (hh + 1) * FOX_HEAD_DIM)

    @pl.when(ki == 0)
    def _():
        q2 = q_ref[0].astype(F32) * (FOX_HEAD_DIM ** -0.5)
        for hh in range(2):
            qa_s[hh] = jnp.where(head_lanes(hh), q2, 0.0).astype(BF16)
        m_s[...] = jnp.full_like(m_s, MASK_VALUE)
        acc_s[...] = jnp.zeros_like(acc_s)

    def step(masked):
        k2 = k_ref[0]
        v2 = v_ref[0].astype(F32)
        if masked:
            row = lax.broadcasted_iota(jnp.int32, (tq, tq), 0)
            col = lax.broadcasted_iota(jnp.int32, (tq, tq), 1)
            keep = col <= row
        for hh in range(2):
            v_aug = jnp.where(head_lanes(hh), v2, 1.0).astype(BF16)
            s = _dot_nt(qa_s[hh], k2) - fk_ref[0, hh]
            if masked:
                s = jnp.where(keep, s, MASK_VALUE)
            m_prev = m_s[hh]
            m_next = jnp.maximum(m_prev, jnp.max(s, axis=1, keepdims=True))
            alpha = jnp.exp(m_prev - m_next)
            pr = jnp.exp(s - jnp.tile(m_next, (1, tq // LANES)))
            acc_s[hh] = alpha * acc_s[hh] + _dot(pr.astype(BF16), v_aug)
            m_s[hh] = m_next

    @pl.when(ki < qi)
    def _():
        step(False)

    @pl.when(ki == qi)
    def _():
        step(True)
        a0, a1 = acc_s[0], acc_s[1]
        o0 = a0 / pltpu.roll(a0, FOX_HEAD_DIM, 1)
        o1 = a1 / pltpu.roll(a1, FOX_HEAD_DIM, 1)
        o_ref[0] = jnp.where(lane < FOX_HEAD_DIM, o0, o1).astype(o_ref.dtype)


def _fox_prompt(q, k, v, fcum_t, batch, seq):
    tq = FOX_TQ
    n_q = seq // tq
    pairs = [(i, j) for i in range(n_q) for j in range(i + 1)]
    qi_tab = jnp.asarray([p[0] for p in pairs], jnp.int32)
    ki_tab = jnp.asarray([p[1] for p in pairs], jnp.int32)
    n_hp = FOX_HEADS // 2
    grid_spec = pltpu.PrefetchScalarGridSpec(
        num_scalar_prefetch=2,
        grid=(batch, n_hp, len(pairs)),
        in_specs=[
            pl.BlockSpec((1, tq, LANES), lambda b, h, p, qi, ki: (b, qi[p], h)),
            pl.BlockSpec((1, tq, LANES), lambda b, h, p, qi, ki: (b, ki[p], h)),
            pl.BlockSpec((1, tq, LANES), lambda b, h, p, qi, ki: (b, ki[p], h)),
            pl.BlockSpec((1, 2, 1, tq), lambda b, h, p, qi, ki: (b, h, 0, ki[p])),
        ],
        out_specs=pl.BlockSpec((1, tq, LANES), lambda b, h, p, qi, ki: (b, qi[p], h)),
        scratch_shapes=[pltpu.VMEM((2, tq, LANES), BF16), pltpu.VMEM((2, tq, LANES), F32),
                        pltpu.VMEM((2, tq, LANES), F32)],
    )
    return pl.pallas_call(
        _fox_prompt_body,
        grid_spec=grid_spec,
        out_shape=jax.ShapeDtypeStruct((batch, seq, FOX_HEADS * FOX_HEAD_DIM), BF16),
        compiler_params=_cparams(("parallel", "parallel", "arbitrary")),
        name="fox_prompt",
    )(qi_tab, ki_tab, q, k, v, fcum_t)


DEC_PAGES = 8


def _fox_decode_body(pt_ref, q_ref, kn_ref, vn_ref, bias_ref, *rest):
    k_refs, v_refs = rest[:DEC_PAGES], rest[DEC_PAGES:2 * DEC_PAGES]
    o_ref, m_s, l_s, acc_s = rest[2 * DEC_PAGES:]
    g = pl.program_id(1)
    width = FOX_HEADS * FOX_HEAD_DIM
    page = k_refs[0].shape[-1]
    lane = lax.broadcasted_iota(jnp.int32, (FOX_HEADS, width), 1)
    row = lax.broadcasted_iota(jnp.int32, (FOX_HEADS, width), 0)
    own = (lane >= row * FOX_HEAD_DIM) & (lane < (row + 1) * FOX_HEAD_DIM)
    scale = FOX_HEAD_DIM ** -0.5
    q_row = jnp.broadcast_to(q_ref[0].astype(F32), (FOX_HEADS, width))
    q_blk = jnp.where(own, q_row, 0.0).astype(BF16)

    @pl.when(g == 0)
    def _():
        m_s[...] = jnp.full_like(m_s, MASK_VALUE)
        l_s[...] = jnp.zeros_like(l_s)
        acc_s[...] = jnp.zeros_like(acc_s)

    k_t = jnp.concatenate([r[0].reshape(width, page).astype(BF16) for r in k_refs], axis=1)
    s = _dot(q_blk, k_t) * scale + bias_ref[0]
    m_prev = m_s[...]
    m_next = jnp.maximum(m_prev, jnp.max(s, axis=-1, keepdims=True))
    alpha = jnp.exp(m_prev - m_next)
    pr = jnp.exp(s - m_next)
    l_s[...] = alpha * l_s[...] + jnp.sum(pr, axis=-1, keepdims=True)
    prb = pr.astype(BF16)
    pv = _dot_nt(prb[:, 0:page], v_refs[0][0].reshape(width, page).astype(BF16))
    for i in range(1, DEC_PAGES):
        pv = pv + _dot_nt(prb[:, i * page:(i + 1) * page], v_refs[i][0].reshape(width, page).astype(BF16))
    acc_s[...] = alpha * acc_s[...] + pv
    m_s[...] = m_next

    @pl.when(g == pl.num_programs(1) - 1)
    def _():
        knb = kn_ref[0].astype(BF16).astype(F32)
        s_new = jnp.sum(q_blk.astype(F32) * knb, axis=-1, keepdims=True) * scale
        m_fin = jnp.maximum(m_next, s_new)
        a = jnp.exp(m_next - m_fin)
        p_new = jnp.exp(s_new - m_fin)
        denom = a * l_s[...] + p_new
        vnb = vn_ref[0].astype(BF16).astype(F32)
        o_full = (a * acc_s[...] + p_new.astype(BF16).astype(F32) * vnb) / denom
        o_ref[0] = jnp.sum(jnp.where(own, o_full, 0.0), axis=0, keepdims=True).astype(o_ref.dtype)


def _fox_decode(q, k_new, v_new, cache_kt, cache_vt, page_table, bias_t):
    b, width = q.shape
    n_pages = page_table.shape[1]
    _, heads, hd, page = cache_kt.shape
    n_groups = n_pages // DEC_PAGES
    vec = pl.BlockSpec((1, 1, width), lambda i, g, pt: (i, 0, 0))

    def page_spec(j):
        return pl.BlockSpec((1, heads, hd, page), lambda i, g, pt: (pt[i, g * DEC_PAGES + j], 0, 0, 0))

    grid_spec = pltpu.PrefetchScalarGridSpec(
        num_scalar_prefetch=1,
        grid=(b, n_groups),
        in_specs=[vec, vec, vec,
                  pl.BlockSpec((1, FOX_HEADS, DEC_PAGES * page), lambda i, g, pt: (i, 0, g))]
                 + [page_spec(j) for j in range(DEC_PAGES)] * 2,
        out_specs=vec,
        scratch_shapes=[pltpu.VMEM((FOX_HEADS, 1), F32), pltpu.VMEM((FOX_HEADS, 1), F32),
                        pltpu.VMEM((FOX_HEADS, width), F32)],
    )
    out = pl.pallas_call(
        _fox_decode_body,
        grid_spec=grid_spec,
        out_shape=jax.ShapeDtypeStruct((b, 1, width), BF16),
        compiler_params=_cparams(("parallel", "arbitrary")),
        name="fox_decode",
    )(page_table, q.reshape(b, 1, width), k_new.reshape(b, 1, width), v_new.reshape(b, 1, width), bias_t,
      *([cache_kt] * DEC_PAGES), *([cache_vt] * DEC_PAGES))
    return out.reshape(b, width)


def kernel(x_prompt, x_sample, state_ret, cache_k, cache_v, cache_logf, page_table, c_prompt, c_sample,
           norm_mix, norm_ffn, ada_w, ada_b, ret_w_in, ret_w_out, fox_w_q, fox_w_o, norm_kv, ada_kv_w,
           ada_kv_b, kv_w, f_w, f_b, router_w, router_b, expert_w_gu, expert_b_gu, expert_w_down,
           expert_b_down, norm_final, ada_final_w, ada_final_b):
    d = D_MODEL
    bp, seq, _ = x_prompt.shape
    bs = x_sample.shape[0]
    tp = bp * seq
    past_len = page_table.shape[1] * cache_k.shape[1]

    ret_w_in_b = ret_w_in.astype(BF16)
    ret_w_out_b = ret_w_out.astype(BF16)
    fox_w_q_b = fox_w_q.astype(BF16)
    fox_w_o_b = fox_w_o.astype(BF16)
    kv_w_b = kv_w.astype(BF16)
    f_w_pad = jnp.zeros((d, LANES), BF16).at[:, :FOX_HEADS].set(f_w.astype(BF16))
    f_b_pad = jnp.zeros((LANES,), F32).at[:FOX_HEADS].set(f_b)
    router_w_pad = jnp.zeros((DEPTH, d, LANES), BF16).at[:, :, :N_EXPERTS].set(router_w.astype(BF16))
    router_b_pad = jnp.full((DEPTH, 1, LANES), -jnp.inf, F32).at[:, 0, :N_EXPERTS].set(router_b)
    w1 = expert_w_gu.reshape(DEPTH * N_EXPERTS, d, 2 * D_FF)
    b1 = _regroup_gu_bias(expert_b_gu).reshape(DEPTH * N_EXPERTS, 1, 2 * D_FF)
    w2 = expert_w_down.reshape(DEPTH * N_EXPERTS, D_FF, d)
    b2 = expert_b_down.reshape(DEPTH * N_EXPERTS, 1, d)

    ada_all_w = jnp.concatenate([ada_w[l] for l in range(DEPTH)] + [ada_kv_w, ada_final_w], axis=1).astype(BF16)
    ada_all_b = jnp.concatenate([ada_b[l] for l in range(DEPTH)] + [ada_kv_b, ada_final_b], axis=0)
    n_c = bs + bp
    n_c_pad = ((n_c + 7) // 8) * 8
    c_all = jnp.zeros((n_c_pad, d), F32).at[:bs].set(c_sample).at[bs:n_c].set(c_prompt)
    mods = _linear(c_all, ada_all_w, pre_silu=True, bias=ada_all_b, tm=n_c_pad, tn=2048, name="ada_params")

    def mod_s(col):
        return mods[:bs, col * d:(col + 1) * d]

    def mod_p(col):
        return mods[bs:n_c, col * d:(col + 1) * d].reshape(bp, 1, d)

    xp = x_prompt.reshape(tp, d)
    xs = x_sample.reshape(bs, d)
    sample_pos = past_len
    cache_kt = jnp.transpose(cache_k, (0, 2, 3, 1))
    cache_vt = jnp.transpose(cache_v, (0, 2, 3, 1))

    ret_states_p, ret_states_s = [], None
    kv_p = kv_s = None
    for layer in range(DEPTH):
        base = layer * N_MOD
        gain_mix = norm_mix[layer]
        mix_mod_p = (gain_mix, mod_p(base + 0), mod_p(base + 1))
        mix_mod_s = (gain_mix, mod_s(base + 0), mod_s(base + 1))
        if layer < N_A_LAYERS:
            proj_p = _linear(xp, ret_w_in_b[layer], mod=mix_mod_p, out_dtype=BF16, tn=2048,
                             rows_per_group=seq, name="ret_in_prompt")
            gated_p, st_p = _retention_prompt(proj_p, bp, seq)
            xp = _linear(gated_p, ret_w_out_b[layer], epi=(xp, mod_p(base + 2)), rows_per_group=seq,
                         name="ret_out_prompt")
            ret_states_p.append(st_p)
            proj_s = _linear(xs, ret_w_in_b[layer], mod=mix_mod_s, out_dtype=F32, tn=2048, name="ret_in_sample")
            gated_s, ret_states_s = _retention_sample(proj_s, state_ret, layer, sample_pos, ret_states_s)
            xs = _linear(gated_s, ret_w_out_b[layer], epi=(xs, mod_s(base + 2)), name="ret_out_sample")
        else:
            j = layer - N_A_LAYERS
            k_p, v_p, fcum_t, k_pb, v_pb = kv_p
            q_p = _linear(xp, fox_w_q_b[j], mod=mix_mod_p, out_dtype=BF16, rows_per_group=seq, name="fox_q_prompt")
            att_p = _fox_prompt(q_p.reshape(bp, seq, d), k_pb, v_pb, fcum_t, bp, seq)
            xp = _linear(att_p.reshape(tp, d), fox_w_o_b[j], epi=(xp, mod_p(base + 2)), rows_per_group=seq,
                         name="fox_o_prompt")
            k_s, v_s, bias_t = kv_s
            q_s = _linear(xs, fox_w_q_b[j], mod=mix_mod_s, out_dtype=F32, name="fox_q_sample")
            att_s = _fox_decode(q_s, k_s, v_s, cache_kt, cache_vt, page_table, bias_t)
            xs = _linear(att_s, fox_w_o_b[j], epi=(xs, mod_s(base + 2)), name="fox_o_sample")

        gain_ffn = norm_ffn[layer]
        buf_rows = _moe_sorted_rows(tp + bs)
        bufs, cnt_p = _router(xp, gain_ffn, mod_p(base + 3), mod_p(base + 4), router_w_pad[layer],
                              router_b_pad[layer], tm=ROUTER_TILE, total_rows=buf_rows, rows_per_group=seq)
        (h_all, ti_all, tw_all), cnt = _router(xs, gain_ffn, mod_s(base + 3), mod_s(base + 4), router_w_pad[layer],
                                               router_b_pad[layer], tm=bs, total_rows=buf_rows, row_offset=tp,
                                               bufs=bufs, counts=cnt_p)
        xp, xs = _moe_layer([xp, xs], h_all, ti_all, tw_all, cnt, [mod_p(base + 5), mod_s(base + 5)], [seq, None],
                            layer * N_EXPERTS, w1, b1, w2, b2)

        if layer == N_A_LAYERS - 1:
            kvb = DEPTH * N_MOD
            kv_mod_p = (norm_kv, mod_p(kvb + 0), mod_p(kvb + 1))
            kv_mod_s = (norm_kv, mod_s(kvb + 0), mod_s(kvb + 1))
            kvp = _linear(xp, kv_w_b, mod=kv_mod_p, rows_per_group=seq, name="kv_prompt")
            lf_p = _linear(xp, f_w_pad, mod=kv_mod_p, bias=f_b_pad, post="log_sigmoid", rows_per_group=seq,
                           name="logf_prompt")[:, :FOX_HEADS]
            k_p, v_p = kvp[:, :d], kvp[:, d:]
            logf_p = lf_p.reshape(bp, seq, FOX_HEADS)
            fcum_t = jnp.transpose(jnp.cumsum(logf_p, axis=1), (0, 2, 1))[:, :, None, :]
            kv_p = (k_p, v_p, fcum_t, k_p.astype(BF16).reshape(bp, seq, d), v_p.astype(BF16).reshape(bp, seq, d))

            kvs = _linear(xs, kv_w_b, mod=kv_mod_s, name="kv_sample")
            lf_s = _linear(xs, f_w_pad, mod=kv_mod_s, bias=f_b_pad, post="log_sigmoid",
                           name="logf_sample")[:, :FOX_HEADS]
            k_s, v_s = kvs[:, :d], kvs[:, d:]
            lf_past = cache_logf[page_table].reshape(bs, past_len, FOX_HEADS).astype(F32)
            cs = jnp.cumsum(lf_past, axis=1)
            bias = (cs[:, -1:, :] - cs) + lf_s[:, None, :]
            kv_s = (k_s, v_s, jnp.transpose(bias, (0, 2, 1)))

    fb = DEPTH * N_MOD + 2
    y_p = _linear(xp, None, mod=(norm_final, mod_p(fb + 0), mod_p(fb + 1)), rows_per_group=seq, name="final_prompt")
    y_s = _linear(xs, None, mod=(norm_final, mod_s(fb + 0), mod_s(fb + 1)), name="final_sample")

    k_p, v_p = kv_p[0], kv_p[1]
    k_s, v_s = kv_s[0], kv_s[1]
    return (y_p.reshape(bp, seq, d),
            y_s.reshape(bs, 1, d),
            jnp.stack(ret_states_p),
            ret_states_s,
            k_p.reshape(bp, seq, FOX_HEADS, FOX_HEAD_DIM),
            v_p.reshape(bp, seq, FOX_HEADS, FOX_HEAD_DIM),
            logf_p,
            k_s.reshape(bs, 1, FOX_HEADS, FOX_HEAD_DIM),
            v_s.reshape(bs, 1, FOX_HEADS, FOX_HEAD_DIM),
            lf_s.reshape(bs, 1, FOX_HEADS))
```
